```python
import jax, jax.numpy as jnp
from jax import lax
import numpy as np

D_MODEL = 2048
BATCH = 4
SEQ = 2048
DEPTH = 4
DEC_BATCH = 128
DEC_SEQ = 4
PAST_LEN = 16384
PAGE_SIZE = 128

SGU_WIDTH = D_MODEL
SGU_CHUNK = 128
SGU_GROUP_DIM = 128
SGU_GROUPS = SGU_WIDTH // SGU_GROUP_DIM
SSM_EXPAND = 2
D_INNER = SSM_EXPAND * D_MODEL
SSM_HEAD_DIM = 64
SSM_HEADS = D_INNER // SSM_HEAD_DIM
SSM_STATE = 128
SSM_GROUPS = 8
CONV_WIDTH = 4
CONV_DIM = D_INNER + 2 * SSM_GROUPS * SSM_STATE
SSD_CHUNK = 128
FFN_HIDDEN = -(-8 * D_MODEL // (3 * 256)) * 256
EPS = 1e-6
IN_SIZES = (2 * SGU_WIDTH, D_INNER, CONV_DIM, SSM_HEADS, D_MODEL, D_MODEL)
IN_COLS = sum(IN_SIZES)

kernel_name = "gated_gmlp_ssd_hybrid_step"


def rmsnorm(x, g):
    xf = x.astype(jnp.float32)
    y = xf * lax.rsqrt(jnp.mean(xf * xf, axis=-1, keepdims=True) + EPS)
    return (y * g.astype(jnp.float32)).astype(x.dtype)


def layernorm(x, g, b):
    xf = x.astype(jnp.float32)
    mu = jnp.mean(xf, axis=-1, keepdims=True)
    var = jnp.mean(jnp.square(xf - mu), axis=-1, keepdims=True)
    y = (xf - mu) * lax.rsqrt(var + EPS)
    return (y * g.astype(jnp.float32) + b.astype(jnp.float32)).astype(x.dtype)


def sgu_branch(s_in, ln_g, ln_b, w_s, b_s):
    u, v = jnp.split(jax.nn.gelu(s_in), 2, axis=-1)
    v = layernorm(v, ln_g, ln_b)
    bsz, L, _ = v.shape
    t = min(SGU_CHUNK, L)
    assert L % t == 0
    vc = v.reshape(bsz, L // t, t, SGU_GROUPS, SGU_GROUP_DIM)
    mask = jnp.tril(jnp.ones((t, t), dtype=bool))
    w = jnp.where(mask, w_s[:, :t, :t], 0).astype(v.dtype)
    bias = jnp.transpose(b_s[:, :t])[None, None, :, :, None].astype(v.dtype)
    mixed = jnp.einsum("gts,bcsgd->bctgd", w, vc) + bias
    return u * mixed.reshape(bsz, L, SGU_WIDTH), v


def ssd(x, dt, a, b_in, c_in, h0):
    bsz, L, H, P = x.shape
    G, N = b_in.shape[2], b_in.shape[3]
    R = H // G
    q = min(SSD_CHUNK, L)
    assert L % q == 0
    nc = L // q
    f32 = jnp.float32
    xdt = (x.astype(f32) * dt[..., None]).reshape(bsz, nc, q, G, R, P)
    da = (dt * a).reshape(bsz, nc, q, G, R).transpose(0, 3, 4, 1, 2)
    cs = jnp.cumsum(da, axis=-1)
    bc = b_in.astype(f32).reshape(bsz, nc, q, G, N)
    cc = c_in.astype(f32).reshape(bsz, nc, q, G, N)
    causal = jnp.tril(jnp.ones((q, q), dtype=bool))
    seg = cs[..., :, None] - cs[..., None, :]
    lmat = jnp.where(causal, jnp.exp(jnp.where(causal, seg, 0.0)), 0.0)
    cb = jnp.einsum("bclgn,bcsgn->bcgls", cc, bc)
    y_diag = jnp.einsum("bcgls,bgrcls,bcsgrp->bclgrp", cb, lmat, xdt)
    decay_s = jnp.exp(cs[..., -1:] - cs)
    states = jnp.einsum("bclgn,bgrcl,bclgrp->bcgrpn", bc, decay_s, xdt)
    chunk_decay = jnp.exp(cs[..., -1])

    def step(h, inp):
        s, dec = inp
        return dec[..., None, None] * h + s, h

    h_init = h0.astype(f32).reshape(bsz, G, R, P, N)
    h_fin, prev = lax.scan(step, h_init, (jnp.moveaxis(states, 1, 0), jnp.moveaxis(chunk_decay, -1, 0)))
    prev = jnp.moveaxis(prev, 0, 1)
    y_off = jnp.einsum("bclgn,bcgrpn,bgrcl->bclgrp", cc, prev, jnp.exp(cs))
    y = (y_diag + y_off).reshape(bsz, L, H, P)
    return y, h_fin.reshape(bsz, H, P, N).astype(h0.dtype)


def mamba_branch(z, xbc, dt_raw, conv_buf, h0, conv_w, conv_b, dt_bias, a_log, d_skip, norm_g):
    bsz, L, _ = xbc.shape
    xpad = jnp.concatenate([conv_buf.astype(xbc.dtype), xbc], axis=1)
    conv = conv_b.astype(xbc.dtype) + sum(conv_w[k].astype(xbc.dtype) * xpad[:, k:k + L] for k in range(CONV_WIDTH))
    conv = jax.nn.silu(conv)
    new_buf = xpad[:, -(CONV_WIDTH - 1):]
    xs, bm, cm = jnp.split(conv, [D_INNER, D_INNER + SSM_GROUPS * SSM_STATE], axis=-1)
    xs = xs.reshape(bsz, L, SSM_HEADS, SSM_HEAD_DIM)
    bm = bm.reshape(bsz, L, SSM_GROUPS, SSM_STATE)
    cm = cm.reshape(bsz, L, SSM_GROUPS, SSM_STATE)
    dt = jax.nn.softplus(dt_raw.astype(jnp.float32) + dt_bias.astype(jnp.float32))
    a = -jnp.exp(a_log.astype(jnp.float32))
    y, h_new = ssd(xs, dt, a, bm, cm, h0)
    y = y + d_skip.astype(jnp.float32)[None, None, :, None] * xs.astype(jnp.float32)
    y = y.reshape(bsz, L, D_INNER).astype(z.dtype) * jax.nn.silu(z)
    gsz = D_INNER // SSM_GROUPS
    y = rmsnorm(y.reshape(bsz, L, SSM_GROUPS, gsz), norm_g.reshape(SSM_GROUPS, gsz)).reshape(bsz, L, D_INNER)
    return y, h_new, new_buf


def layer(x, h0, conv_buf, norm1_g, w_in, conv_w, conv_b, dt_bias, a_log, d_skip, ssm_norm_g,
          sgu_ln_g, sgu_ln_b, sgu_w, sgu_b, w_out_a, w_out_b, w_o, norm2_g, w_ffn_gate, w_ffn_up, w_ffn_down):
    h = rmsnorm(x, norm1_g)
    proj = h @ w_in
    s_in, z, xbc, dt_raw, ga, gb = jnp.split(proj, list(np.cumsum(IN_SIZES)[:-1]), axis=-1)
    ya, v_rows = sgu_branch(s_in, sgu_ln_g, sgu_ln_b, sgu_w, sgu_b)
    yb, h_new, new_buf = mamba_branch(z, xbc, dt_raw, conv_buf, h0, conv_w, conv_b, dt_bias, a_log, d_skip, ssm_norm_g)
    merged = jax.nn.sigmoid(ga) * (ya @ w_out_a) + jax.nn.sigmoid(gb) * (yb @ w_out_b)
    x = x + merged @ w_o
    h2 = rmsnorm(x, norm2_g)
    x = x + (jax.nn.silu(h2 @ w_ffn_gate) * (h2 @ w_ffn_up)) @ w_ffn_down
    return x, h_new, new_buf, v_rows


def setup_inputs(seed: int = 0) -> dict:
    key = jax.random.key(seed)
    ks = jax.random.split(key, 24)
    nrm = jax.random.normal
    f32 = jnp.float32
    dt0 = jnp.exp(jax.random.uniform(ks[9], (DEPTH, SSM_HEADS), f32, np.log(1e-3), np.log(1e-1)))
    return {
        "x_prompt": nrm(ks[0], (BATCH, SEQ, D_MODEL), f32),
        "x_sample": nrm(ks[1], (DEC_BATCH, DEC_SEQ, D_MODEL), f32),
        "state_ssm": 0.1 * nrm(ks[2], (DEPTH, DEC_BATCH, SSM_HEADS, SSM_HEAD_DIM, SSM_STATE), f32),
        "state_conv": nrm(ks[3], (DEPTH, DEC_BATCH, CONV_WIDTH - 1, CONV_DIM), f32),
        "norm1_g": 1.0 + 0.02 * nrm(ks[4], (DEPTH, D_MODEL), f32),
        "w_in": nrm(ks[5], (DEPTH, D_MODEL, IN_COLS), f32) * D_MODEL ** -0.5,
        "conv_w": nrm(ks[6], (DEPTH, CONV_WIDTH, CONV_DIM), f32) * CONV_WIDTH ** -0.5,
        "conv_b": 0.02 * nrm(ks[7], (DEPTH, CONV_DIM), f32),
        "dt_bias": dt0 + jnp.log(-jnp.expm1(-dt0)),
        "a_log": jnp.log(jax.random.uniform(ks[10], (DEPTH, SSM_HEADS), f32, 1.0, 16.0)),
        "d_skip": 1.0 + 0.02 * nrm(ks[11], (DEPTH, SSM_HEADS), f32),
        "ssm_norm_g": 1.0 + 0.02 * nrm(ks[12], (DEPTH, D_INNER), f32),
        "sgu_ln_g": 1.0 + 0.02 * nrm(ks[13], (DEPTH, SGU_WIDTH), f32),
        "sgu_ln_b": 0.02 * nrm(ks[14], (DEPTH, SGU_WIDTH), f32),
        "sgu_w": nrm(ks[15], (DEPTH, SGU_GROUPS, SGU_CHUNK, SGU_CHUNK), f32) * SGU_CHUNK ** -0.5,
        "sgu_b": 1.0 + 0.02 * nrm(ks[16], (DEPTH, SGU_GROUPS, SGU_CHUNK), f32),
        "w_out_a": nrm(ks[17], (DEPTH, SGU_WIDTH, D_MODEL), f32) * SGU_WIDTH ** -0.5,
        "w_out_b": nrm(ks[18], (DEPTH, D_INNER, D_MODEL), f32) * D_INNER ** -0.5,
        "w_o": nrm(ks[19], (DEPTH, D_MODEL, D_MODEL), f32) * D_MODEL ** -0.5,
        "norm2_g": 1.0 + 0.02 * nrm(ks[20], (DEPTH, D_MODEL), f32),
        "w_ffn_gate": nrm(ks[21], (DEPTH, D_MODEL, FFN_HIDDEN), f32) * D_MODEL ** -0.5,
        "w_ffn_up": nrm(ks[22], (DEPTH, D_MODEL, FFN_HIDDEN), f32) * D_MODEL ** -0.5,
        "w_ffn_down": nrm(ks[23], (DEPTH, FFN_HIDDEN, D_MODEL), f32) * FFN_HIDDEN ** -0.5,
        "final_norm_g": 1.0 + 0.02 * nrm(ks[8], (D_MODEL,), f32),
    }


def reference(x_prompt, x_sample, state_ssm, state_conv, norm1_g, w_in, conv_w, conv_b, dt_bias, a_log,
              d_skip, ssm_norm_g, sgu_ln_g, sgu_ln_b, sgu_w, sgu_b, w_out_a, w_out_b, w_o, norm2_g,
              w_ffn_gate, w_ffn_up, w_ffn_down, final_norm_g):
    xp, xs = x_prompt, x_sample
    bp = x_prompt.shape[0]
    h0_prompt = jnp.zeros((bp, SSM_HEADS, SSM_HEAD_DIM, SSM_STATE), state_ssm.dtype)
    buf_prompt = jnp.zeros((bp, CONV_WIDTH - 1, CONV_DIM), state_conv.dtype)
    ssm_p, conv_p, ssm_s, conv_s, v_s = [], [], [], [], []
    for l in range(DEPTH):
        w = (norm1_g[l], w_in[l], conv_w[l], conv_b[l], dt_bias[l], a_log[l], d_skip[l], ssm_norm_g[l],
             sgu_ln_g[l], sgu_ln_b[l], sgu_w[l], sgu_b[l], w_out_a[l], w_out_b[l], w_o[l], norm2_g[l],
             w_ffn_gate[l], w_ffn_up[l], w_ffn_down[l])
        xp, hp, bufp, _ = layer(xp, h0_prompt, buf_prompt, *w)
        xs, hs, bufs, vs = layer(xs, state_ssm[l], state_conv[l], *w)
        ssm_p.append(hp)
        conv_p.append(bufp)
        ssm_s.append(hs)
        conv_s.append(bufs)
        v_s.append(vs)
    y_prompt = rmsnorm(xp, final_norm_g)
    y_sample = rmsnorm(xs, final_norm_g)
    return (y_prompt, y_sample, jnp.stack(ssm_p), jnp.stack(conv_p), jnp.stack(ssm_s), jnp.stack(conv_s), jnp.stack(v_s))
```

```python
import functools

import jax
import jax.numpy as jnp
from jax import lax
from jax.experimental import pallas as pl
from jax.experimental.pallas import tpu as pltpu

F32 = jnp.float32
BF16 = jnp.bfloat16
EPS = 1e-6

LANES = 128
SUBLANES = 8
VMEM_CAP_BYTES = 64 * 1024 * 1024

SGU_CHUNK = 128
SGU_GROUP_DIM = 128
SSM_HEAD_DIM = 64
SSM_STATE = 128
SSM_GROUPS = 8
CONV_WIDTH = 4
SSD_CHUNK = 128


def _vmem_limit(estimate_bytes):
    return int(min(estimate_bytes + (8 << 20), VMEM_CAP_BYTES - (6 << 20)))


def _silu(x):
    return x * jax.nn.sigmoid(x)


def _softplus(x):
    return jnp.maximum(x, 0.0) + jnp.log1p(jnp.exp(-jnp.abs(x)))


def _rmsnorm(x, g):
    return x * lax.rsqrt(jnp.mean(x * x, axis=-1, keepdims=True) + EPS) * g


def _dot(a, b):
    return jnp.dot(a, b, preferred_element_type=F32)


def _dot_nt(a, b):
    return lax.dot_general(a, b, (((1,), (1,)), ((), ())), preferred_element_type=F32)


def _dot_tn(a, b):
    return lax.dot_general(a, b, (((0,), (0,)), ((), ())), preferred_element_type=F32)


def _in_proj_body(x_ref, g_ref, w_ref, wdt_ref, dtb_ref, su_ref, z_ref, xbc_ref, gt_ref, dt_ref,
                  h_scr, *, n_su, n_z, n_xbc):
    n = pl.program_id(1)

    @pl.when(n == 0)
    def _():
        hb = _rmsnorm(x_ref[...], g_ref[...]).astype(BF16)
        h_scr[...] = hb
        dt_ref[...] = _softplus(_dot(hb, wdt_ref[...]) + dtb_ref[...])

    acc = _dot(h_scr[...], w_ref[...])

    @pl.when(n < n_su)
    def _():
        su_ref[...] = jax.nn.gelu(acc).astype(su_ref.dtype)

    @pl.when((n >= n_su) & (n < n_z))
    def _():
        z_ref[...] = acc.astype(z_ref.dtype)

    @pl.when((n >= n_z) & (n < n_xbc))
    def _():
        xbc_ref[...] = acc.astype(xbc_ref.dtype)

    @pl.when(n >= n_xbc)
    def _():
        gt_ref[...] = jax.nn.sigmoid(acc).astype(gt_ref.dtype)


def _in_proj(x, g, w_main, w_dt, dt_b, seg_widths, act_dtype, tm, tn):
    m, d = x.shape
    w_su, w_z, w_xbc, w_gt = seg_widths
    c_su, c_z, c_xbc, c_gt = w_su // tn, w_z // tn, w_xbc // tn, w_gt // tn
    n_su, n_z, n_xbc = c_su, c_su + c_z, c_su + c_z + c_xbc
    n_tiles = n_xbc + c_gt
    dtw = w_dt.shape[1]

    def seg_map(start, count):
        return lambda i, j: (i, jnp.clip(j - start, 0, count - 1))

    est = 2 * (tm * d * 4 + d * tn * 2 + d * dtw * 2 + 4 * tm * tn * 4 + tm * dtw * 4) + tm * d * 2 + tm * tn * 4
    return pl.pallas_call(
        functools.partial(_in_proj_body, n_su=n_su, n_z=n_z, n_xbc=n_xbc),
        grid=(m // tm, n_tiles),
        in_specs=[
            pl.BlockSpec((tm, d), lambda i, j: (i, 0)),
            pl.BlockSpec((1, d), lambda i, j: (0, 0)),
            pl.BlockSpec((d, tn), lambda i, j: (0, j)),
            pl.BlockSpec((d, dtw), lambda i, j: (0, 0)),
            pl.BlockSpec((1, dtw), lambda i, j: (0, 0)),
        ],
        out_specs=[
            pl.BlockSpec((tm, tn), seg_map(0, c_su)),
            pl.BlockSpec((tm, tn), seg_map(n_su, c_z)),
            pl.BlockSpec((tm, tn), seg_map(n_z, c_xbc)),
            pl.BlockSpec((tm, tn), seg_map(n_xbc, c_gt)),
            pl.BlockSpec((tm, dtw), lambda i, j: (i, 0)),
        ],
        out_shape=[
            jax.ShapeDtypeStruct((m, w_su), act_dtype),
            jax.ShapeDtypeStruct((m, w_z), act_dtype),
            jax.ShapeDtypeStruct((m, w_xbc), act_dtype),
            jax.ShapeDtypeStruct((m, w_gt), act_dtype),
            jax.ShapeDtypeStruct((m, dtw), F32),
        ],
        scratch_shapes=[pltpu.VMEM((tm, d), BF16)],
        compiler_params=pltpu.CompilerParams(
            dimension_semantics=("arbitrary", "arbitrary"), vmem_limit_bytes=_vmem_limit(est)),
        name="in_proj",
    )(x, g, w_main, w_dt, dt_b)


def _dt_expand_body(x_ref, g_ref, w_ref, b_ref, o_ref):
    hb = _rmsnorm(x_ref[...], g_ref[...]).astype(BF16)
    o_ref[...] = _softplus(_dot(hb, w_ref[...]) + b_ref[...])


def _dt_expand(x, g, w_dt_exp, dt_b_exp, tn):
    m, d = x.shape
    n = w_dt_exp.shape[1]
    est = 2 * (m * d * 4 + d * tn * 2 + m * tn * 4) + m * d * 6
    return pl.pallas_call(
        _dt_expand_body,
        grid=(n // tn,),
        in_specs=[
            pl.BlockSpec((m, d), lambda j: (0, 0)),
            pl.BlockSpec((1, d), lambda j: (0, 0)),
            pl.BlockSpec((d, tn), lambda j: (0, j)),
            pl.BlockSpec((1, tn), lambda j: (0, j)),
        ],
        out_specs=pl.BlockSpec((m, tn), lambda j: (0, j)),
        out_shape=jax.ShapeDtypeStruct((m, n), F32),
        compiler_params=pltpu.CompilerParams(
            dimension_semantics=("arbitrary",), vmem_limit_bytes=_vmem_limit(est)),
        name="dt_expand",
    )(x, g, w_dt_exp, dt_b_exp)


def _sgu_body(u_ref, v_ref, lng_ref, lnb_ref, mix_ref, bias_ref, wa_ref, *rest, emit_v):
    if emit_v:
        a_ref, vout_ref, ya_scr = rest
    else:
        a_ref, ya_scr = rest
    vg = v_ref[...].astype(F32)
    mu = jnp.mean(vg, axis=-1, keepdims=True)
    var = jnp.mean(jnp.square(vg - mu), axis=-1, keepdims=True)
    v = (vg - mu) * lax.rsqrt(var + EPS) * lng_ref[...] + lnb_ref[...]
    if emit_v:
        vout_ref[...] = v
    vb = v.astype(BF16)
    rows, width = vb.shape
    for c in range(rows // SGU_CHUNK):
        rs = slice(c * SGU_CHUNK, (c + 1) * SGU_CHUNK)
        for grp in range(width // SGU_GROUP_DIM):
            cs = slice(grp * SGU_GROUP_DIM, (grp + 1) * SGU_GROUP_DIM)
            mixed = _dot(mix_ref[grp], vb[rs, cs]) + bias_ref[:, cs]
            ya_scr[rs, cs] = (u_ref[rs, cs].astype(F32) * mixed).astype(BF16)
    a_ref[...] = _dot(ya_scr[...], wa_ref[...]).astype(a_ref.dtype)


def _sgu(su, lng, lnb, mix, bias, w_out_a, rows, emit_v):
    m = su.shape[0]
    width = su.shape[1] // 2
    d = w_out_a.shape[1]
    ngrp = mix.shape[0]
    isz = su.dtype.itemsize
    est = 2 * (2 * rows * width * isz + ngrp * SGU_CHUNK * SGU_CHUNK * 2 + SGU_CHUNK * width * 4
               + rows * d * 2 + rows * width * 4) + width * d * 2 + 4 * rows * width * 4
    out_shape = [jax.ShapeDtypeStruct((m, d), BF16)]
    out_specs = [pl.BlockSpec((rows, d), lambda i: (i, 0))]
    if emit_v:
        out_shape.append(jax.ShapeDtypeStruct((m, width), F32))
        out_specs.append(pl.BlockSpec((rows, width), lambda i: (i, 0)))
    return pl.pallas_call(
        functools.partial(_sgu_body, emit_v=emit_v),
        grid=(m // rows,),
        in_specs=[
            pl.BlockSpec((rows, width), lambda i: (i, 0)),
            pl.BlockSpec((rows, width), lambda i: (i, 1)),
            pl.BlockSpec((1, width), lambda i: (0, 0)),
            pl.BlockSpec((1, width), lambda i: (0, 0)),
            pl.BlockSpec((ngrp, SGU_CHUNK, SGU_CHUNK), lambda i: (0, 0, 0)),
            pl.BlockSpec((SGU_CHUNK, width), lambda i: (0, 0)),
            pl.BlockSpec((width, d), lambda i: (0, 0), pipeline_mode=pl.Buffered(1)),
        ],
        out_specs=out_specs,
        out_shape=out_shape,
        scratch_shapes=[pltpu.VMEM((rows, width), BF16)],
        compiler_params=pltpu.CompilerParams(
            dimension_semantics=("arbitrary",), vmem_limit_bytes=_vmem_limit(est)),
        name="sgu",
    )(su, su, lng, lnb, mix, bias, w_out_a)


def _prefix_sum_rows(x, row):
    n = x.shape[0]
    s = 1
    while s < n:
        x = x + jnp.where(row >= s, pltpu.roll(x, s, 0), 0.0)
        s *= 2
    return x


def _gate_groupnorm(y, z, ng):
    t = y * _silu(z)
    return t * lax.rsqrt(jnp.mean(t * t, axis=-1, keepdims=True) + EPS) * ng


def _ssd_prompt_body(xbc_ref, z_ref, dt_ref, cw_ref, cb_ref, alog_ref, dexp_ref, ng_ref,
                     yb_ref, h_ref, cst_ref, win_scr, act_scr, y_scr, xs_scr, *, d_inner, n_chunks):
    c = pl.program_id(1)
    q = xbc_ref.shape[0]
    conv_dim = xbc_ref.shape[1]
    p = SSM_HEAD_DIM
    n_state = SSM_STATE
    gw = d_inner // SSM_GROUPS
    heads_per_group = gw // p
    halo = SUBLANES

    @pl.when(c == 0)
    def _():
        win_scr[0:halo, :] = jnp.zeros((halo, conv_dim), F32)
        h_ref[...] = jnp.zeros(h_ref.shape, F32)

    win_scr[halo:halo + q, :] = xbc_ref[...].astype(F32)
    lane_chunk = 512
    for j in range(conv_dim // lane_chunk):
        ls = slice(j * lane_chunk, (j + 1) * lane_chunk)
        acc = cb_ref[:, ls] + cw_ref[0:1, ls] * win_scr[halo - 3:halo - 3 + q, ls]
        for k in range(1, CONV_WIDTH):
            acc = acc + cw_ref[k:k + 1, ls] * win_scr[halo - 3 + k:halo - 3 + k + q, ls]
        act_scr[:, ls] = _silu(acc)

    @pl.when(c == n_chunks - 1)
    def _():
        cst_ref[...] = win_scr[halo + q - 3:halo + q, :]

    win_scr[0:halo, :] = win_scr[q:q + halo, :]

    row = lax.broadcasted_iota(jnp.int32, (q, LANES), 0)
    dt = dt_ref[...]
    da = dt * (-jnp.exp(alog_ref[...]))
    cs = _prefix_sum_rows(da, row)
    cs_t = cs.T
    ecs = jnp.exp(cs)
    dec_s = jnp.exp(cs[q - 1:q, :] - cs)
    cd_t = jnp.exp(cs_t[:, q - 1:q])
    li = lax.broadcasted_iota(jnp.int32, (q, q), 0)
    si = lax.broadcasted_iota(jnp.int32, (q, q), 1)
    causal = li >= si

    for grp in range(SSM_GROUPS):
        b_g = act_scr[:, d_inner + grp * n_state:d_inner + (grp + 1) * n_state].astype(BF16)
        c_g = act_scr[:, d_inner + (SSM_GROUPS + grp) * n_state:
                      d_inner + (SSM_GROUPS + grp + 1) * n_state].astype(BF16)
        cb = _dot_nt(c_g, b_g)
        h_g = h_ref[grp * gw:(grp + 1) * gw, :]
        y_off = _dot_nt(c_g, h_g.astype(BF16))
        for r in range(heads_per_group):
            h = grp * heads_per_group + r
            hs = slice(h * p, (h + 1) * p)
            seg = cs[:, h:h + 1] - cs_t[h:h + 1, :]
            lmat = jnp.where(causal, jnp.exp(jnp.where(causal, seg, 0.0)), 0.0)
            w = (cb * lmat).astype(BF16)
            xs_h = act_scr[:, hs]
            xdt = xs_h * dt[:, h:h + 1]
            y_h = (_dot(w, xdt.astype(BF16)) + y_off[:, r * p:(r + 1) * p] * ecs[:, h:h + 1]
                   + dexp_ref[:, hs] * xs_h)
            y_scr[:, hs] = y_h
            xs_scr[:, r * p:(r + 1) * p] = (xdt * dec_s[:, h:h + 1]).astype(BF16)
        states = _dot_tn(xs_scr[...], b_g)
        for r in range(heads_per_group):
            h = grp * heads_per_group + r
            rs = slice(r * p, (r + 1) * p)
            h_ref[grp * gw + r * p:grp * gw + (r + 1) * p, :] = h_g[rs, :] * cd_t[h:h + 1, :] + states[rs, :]

    for grp in range(SSM_GROUPS):
        gs = slice(grp * gw, (grp + 1) * gw)
        yb_ref[:, gs] = _gate_groupnorm(y_scr[:, gs], z_ref[:, gs].astype(F32), ng_ref[:, gs]).astype(yb_ref.dtype)


def _ssd_prompt(xbc, z, dt, conv_w, conv_b, a_log, d_exp, norm_g, batch, seq):
    q = SSD_CHUNK
    n_chunks = seq // q
    conv_dim = xbc.shape[1]
    d_inner = z.shape[1]
    gw = d_inner // SSM_GROUPS
    dtw = dt.shape[1]
    est = (2 * (q * conv_dim * 2 + q * d_inner * 2 + q * dtw * 4 + q * d_inner * 2 + d_inner * SSM_STATE * 4
                + 8 * conv_dim * 4 + 5 * conv_dim * 4 + 2 * d_inner * 4)
           + (q + 8) * conv_dim * 4 + q * conv_dim * 4 + q * d_inner * 4 + q * gw * 2 + (8 << 20))
    row_map = lambda b, c: (b * n_chunks + c, 0)
    const = lambda b, c: (0, 0)
    return pl.pallas_call(
        functools.partial(_ssd_prompt_body, d_inner=d_inner, n_chunks=n_chunks),
        grid=(batch, n_chunks),
        in_specs=[
            pl.BlockSpec((q, conv_dim), row_map),
            pl.BlockSpec((q, d_inner), row_map),
            pl.BlockSpec((q, dtw), row_map),
            pl.BlockSpec((CONV_WIDTH, conv_dim), const),
            pl.BlockSpec((1, conv_dim), const),
            pl.BlockSpec((1, dtw), const),
            pl.BlockSpec((1, d_inner), const),
            pl.BlockSpec((1, d_inner), const),
        ],
        out_specs=[
            pl.BlockSpec((q, d_inner), row_map),
            pl.BlockSpec((None, d_inner, SSM_STATE), lambda b, c: (b, 0, 0)),
            pl.BlockSpec((None, CONV_WIDTH - 1, conv_dim), lambda b, c: (b, 0, 0)),
        ],
        out_shape=[
            jax.ShapeDtypeStruct((batch * seq, d_inner), BF16),
            jax.ShapeDtypeStruct((batch, d_inner, SSM_STATE), F32),
            jax.ShapeDtypeStruct((batch, CONV_WIDTH - 1, conv_dim), F32),
        ],
        scratch_shapes=[
            pltpu.VMEM((q + 2 * SUBLANES, conv_dim), F32),
            pltpu.VMEM((q, conv_dim), F32),
            pltpu.VMEM((q, d_inner), F32),
            pltpu.VMEM((q, gw), BF16),
        ],
        compiler_params=pltpu.CompilerParams(
            dimension_semantics=("arbitrary", "arbitrary"), vmem_limit_bytes=_vmem_limit(est)),
        name="ssd_prompt",
    )(xbc, z, dt, conv_w, conv_b, a_log, d_exp, norm_g)


def _ssd_sample_body(xbc_ref, z_ref, dte_ref, cbuf_ref, h0_ref, cw_ref, cb_ref, aexp_ref, dexp_ref, ng_ref,
                     yb_ref, h_ref, cst_ref, win_scr, pad_scr, *, d_inner):
    steps = xbc_ref.shape[0]
    conv_dim = xbc_ref.shape[1]
    q = SUBLANES
    p = SSM_HEAD_DIM
    n_state = SSM_STATE
    gw = d_inner // SSM_GROUPS
    heads_per_group = gw // p
    halo = SUBLANES

    win_scr[...] = jnp.zeros(win_scr.shape, F32)
    win_scr[halo - 3:halo, :] = cbuf_ref[...]
    win_scr[halo:halo + steps, :] = xbc_ref[...].astype(F32)
    acc = cb_ref[...] + cw_ref[0:1, :] * win_scr[halo - 3:halo - 3 + q, :]
    for k in range(1, CONV_WIDTH):
        acc = acc + cw_ref[k:k + 1, :] * win_scr[halo - 3 + k:halo - 3 + k + q, :]
    act = _silu(acc)
    cst_ref[...] = win_scr[halo + steps - 3:halo + steps, :]

    xs = act[:, :d_inner]
    row = lax.broadcasted_iota(jnp.int32, (q, d_inner), 0)
    pad_scr[...] = jnp.zeros(pad_scr.shape, F32)
    pad_scr[0:steps, :] = dte_ref[...]
    dt = pad_scr[...]
    da = dt * aexp_ref[...]
    cs = _prefix_sum_rows(da, row)
    cs_last = cs[steps - 1:steps, :]
    xdt = xs * dt

    y = dexp_ref[...] * xs
    for s in range(steps):
        cb_parts = []
        for grp in range(SSM_GROUPS):
            b_row = act[s:s + 1, d_inner + grp * n_state:d_inner + (grp + 1) * n_state]
            c_g = act[:, d_inner + (SSM_GROUPS + grp) * n_state:d_inner + (SSM_GROUPS + grp + 1) * n_state]
            cb_parts.append(jnp.broadcast_to(jnp.sum(c_g * b_row, axis=1, keepdims=True), (q, gw)))
        cb_s = jnp.concatenate(cb_parts, axis=1)
        keep = row >= s
        lmat = jnp.where(keep, jnp.exp(jnp.where(keep, cs - cs[s:s + 1, :], 0.0)), 0.0)
        y = y + cb_s * lmat * xdt[s:s + 1, :]

    ecs = jnp.exp(cs)
    xs_dec = (xdt * jnp.exp(cs_last - cs)).astype(BF16)
    cd = jnp.exp(cs_last)
    cd_col = jnp.broadcast_to(cd, (q, d_inner)).T
    for grp in range(SSM_GROUPS):
        gs = slice(grp * gw, (grp + 1) * gw)
        b_g = act[:, d_inner + grp * n_state:d_inner + (grp + 1) * n_state].astype(BF16)
        c_g = act[:, d_inner + (SSM_GROUPS + grp) * n_state:
                  d_inner + (SSM_GROUPS + grp + 1) * n_state].astype(BF16)
        h_g = h0_ref[gs, :]
        y_off = _dot_nt(c_g, h_g.astype(BF16))
        y_g = y[:, gs] + y_off * ecs[:, gs]
        t = _gate_groupnorm(y_g[0:steps, :], z_ref[:, gs].astype(F32), ng_ref[:, gs])
        yb_ref[:, gs] = t.astype(yb_ref.dtype)
        states = _dot_tn(xs_dec[:, gs], b_g)
        h_ref[gs, :] = h_g * cd_col[gs, 0:1] + states


def _ssd_sample(xbc3, z3, dte3, conv_state, h0_all, layer, conv_w, conv_b, a_exp, d_exp, norm_g):
    nseq, steps, conv_dim = xbc3.shape
    d_inner = z3.shape[2]
    q = SUBLANES
    est = (2 * (2 * d_inner * SSM_STATE * 4 + steps * (conv_dim + 2 * d_inner) * 4 + 3 * conv_dim * 4
                + steps * d_inner * 4 + 3 * conv_dim * 4 + 5 * conv_dim * 4 + 3 * d_inner * 4)
           + 3 * q * conv_dim * 4 + d_inner * LANES * 4 + (12 << 20))
    seq_map = lambda s: (s, 0, 0)
    const = lambda s: (0, 0)
    return pl.pallas_call(
        functools.partial(_ssd_sample_body, d_inner=d_inner),
        grid=(nseq,),
        in_specs=[
            pl.BlockSpec((None, steps, conv_dim), seq_map),
            pl.BlockSpec((None, steps, d_inner), seq_map),
            pl.BlockSpec((None, steps, d_inner), seq_map),
            pl.BlockSpec((None, None, CONV_WIDTH - 1, conv_dim), lambda s: (layer, s, 0, 0)),
            pl.BlockSpec((None, None, d_inner, SSM_STATE), lambda s: (layer, s, 0, 0)),
            pl.BlockSpec((CONV_WIDTH, conv_dim), const),
            pl.BlockSpec((1, conv_dim), const),
            pl.BlockSpec((1, d_inner), const),
            pl.BlockSpec((1, d_inner), const),
            pl.BlockSpec((1, d_inner), const),
        ],
        out_specs=[
            pl.BlockSpec((None, steps, d_inner), seq_map),
            pl.BlockSpec((None, d_inner, SSM_STATE), seq_map),
            pl.BlockSpec((None, CONV_WIDTH - 1, conv_dim), seq_map),
        ],
        out_shape=[
            jax.ShapeDtypeStruct((nseq, steps, d_inner), F32),
            jax.ShapeDtypeStruct((nseq, d_inner, SSM_STATE), F32),
            jax.ShapeDtypeStruct((nseq, CONV_WIDTH - 1, conv_dim), F32),
        ],
        scratch_shapes=[
            pltpu.VMEM((3 * SUBLANES, conv_dim), F32),
            pltpu.VMEM((q, d_inner), F32),
        ],
        compiler_params=pltpu.CompilerParams(
            dimension_semantics=("arbitrary",), vmem_limit_bytes=_vmem_limit(est)),
        name="ssd_sample",
    )(xbc3, z3, dte3, conv_state, h0_all, conv_w, conv_b, a_exp, d_exp, norm_g)


def _merge_body(a_ref, yb_ref, ga_ref, gb_ref, x_ref, wb_ref, wo_ref, o_ref):
    b = _dot(yb_ref[...].astype(BF16), wb_ref[...])
    merged = ga_ref[...].astype(F32) * a_ref[...].astype(F32) + gb_ref[...].astype(F32) * b
    o_ref[...] = x_ref[...] + _dot(merged.astype(BF16), wo_ref[...])


def _merge(a, yb, gates, x, w_out_b, w_o, rows):
    m, d = x.shape
    d_inner = yb.shape[1]
    est = (2 * rows * (d * 2 + d_inner * yb.dtype.itemsize + 2 * d * gates.dtype.itemsize + 2 * d * 4)
           + d_inner * d * 2 + d * d * 2 + 3 * rows * d * 4)
    return pl.pallas_call(
        _merge_body,
        grid=(m // rows,),
        in_specs=[
            pl.BlockSpec((rows, d), lambda i: (i, 0)),
            pl.BlockSpec((rows, d_inner), lambda i: (i, 0)),
            pl.BlockSpec((rows, d), lambda i: (i, 0)),
            pl.BlockSpec((rows, d), lambda i: (i, 1)),
            pl.BlockSpec((rows, d), lambda i: (i, 0)),
            pl.BlockSpec((d_inner, d), lambda i: (0, 0), pipeline_mode=pl.Buffered(1)),
            pl.BlockSpec((d, d), lambda i: (0, 0), pipeline_mode=pl.Buffered(1)),
        ],
        out_specs=pl.BlockSpec((rows, d), lambda i: (i, 0)),
        out_shape=jax.ShapeDtypeStruct((m, d), F32),
        compiler_params=pltpu.CompilerParams(
            dimension_semantics=("arbitrary",), vmem_limit_bytes=_vmem_limit(est)),
        name="merge",
    )(a, yb, gates, gates, x, w_out_b, w_o)


def _ffn_body(x_ref, g_ref, wg_ref, wu_ref, wd_ref, fg_ref, o_ref, h_scr, acc_scr, *, n_f, final_norm):
    f = pl.program_id(1)

    @pl.when(f == 0)
    def _():
        h_scr[...] = _rmsnorm(x_ref[...], g_ref[...]).astype(BF16)
        acc_scr[...] = jnp.zeros(acc_scr.shape, F32)

    hb = h_scr[...]
    act = _silu(_dot(hb, wg_ref[...])) * _dot(hb, wu_ref[...])
    acc_scr[...] += _dot(act.astype(BF16), wd_ref[...])

    @pl.when(f == n_f - 1)
    def _():
        y = x_ref[...] + acc_scr[...]
        if final_norm:
            y = _rmsnorm(y, fg_ref[...])
        o_ref[...] = y


def _ffn(x, g, w_gate, w_up, w_down, final_g, tm, tf, final_norm):
    m, d = x.shape
    hidden = w_gate.shape[1]
    n_f = hidden // tf
    est = 2 * (2 * tm * d * 4 + 3 * d * tf * 2) + tm * d * 6 + 3 * tm * tf * 4
    return pl.pallas_call(
        functools.partial(_ffn_body, n_f=n_f, final_norm=final_norm),
        grid=(m // tm, n_f),
        in_specs=[
            pl.BlockSpec((tm, d), lambda i, j: (i, 0)),
            pl.BlockSpec((1, d), lambda i, j: (0, 0)),
            pl.BlockSpec((d, tf), lambda i, j: (0, j)),
            pl.BlockSpec((d, tf), lambda i, j: (0, j)),
            pl.BlockSpec((tf, d), lambda i, j: (j, 0)),
            pl.BlockSpec((1, d), lambda i, j: (0, 0)),
        ],
        out_specs=pl.BlockSpec((tm, d), lambda i, j: (i, 0)),
        out_shape=jax.ShapeDtypeStruct((m, d), F32),
        scratch_shapes=[pltpu.VMEM((tm, d), BF16), pltpu.VMEM((tm, d), F32)],
        compiler_params=pltpu.CompilerParams(
            dimension_semantics=("arbitrary", "arbitrary"), vmem_limit_bytes=_vmem_limit(est)),
        name="ffn",
    )(x, g, w_gate, w_up, w_down, final_g)


def _row_tile(m, target):
    t = min(m, target)
    while m % t:
        t //= 2
    return t


def kernel(x_prompt, x_sample, state_ssm, state_conv, norm1_g, w_in, conv_w, conv_b, dt_bias, a_log, d_skip, ssm_norm_g, sgu_ln_g, sgu_ln_b, sgu_w, sgu_b, w_out_a, w_out_b, w_o, norm2_g, w_ffn_gate, w_ffn_up, w_ffn_down, final_norm_g):
    batch, seq, d = x_prompt.shape
    nseq, steps, _ = x_sample.shape
    depth = w_in.shape[0]
    heads = dt_bias.shape[1]
    d_inner = heads * SSM_HEAD_DIM
    conv_dim = conv_w.shape[2]
    sgu_width = sgu_ln_g.shape[1]
    ngrp = sgu_w.shape[1]
    seg_widths = (2 * sgu_width, d_inner, conv_dim, 2 * d)
    dt_col = 2 * sgu_width + d_inner + conv_dim

    xp = x_prompt.reshape(batch * seq, d)
    xs = x_sample.reshape(nseq * steps, d)
    fin_g = final_norm_g[None, :]

    tril = jnp.tril(jnp.ones((SGU_CHUNK, SGU_CHUNK), bool))
    seqs_per_chunk = SGU_CHUNK // steps
    eye = jnp.eye(seqs_per_chunk, dtype=F32)
    tril_s = jnp.tril(jnp.ones((steps, steps), bool))

    ssm_p, conv_p, ssm_s, conv_s, v_s = [], [], [], [], []
    for l in range(depth):
        wl = w_in[l]
        w_main = jnp.concatenate([wl[:, :dt_col], wl[:, dt_col + heads:]], axis=1).astype(BF16)
        w_dt = jnp.pad(wl[:, dt_col:dt_col + heads], ((0, 0), (0, LANES - heads))).astype(BF16)
        dt_b = jnp.pad(dt_bias[l], (0, LANES - heads))[None, :]
        w_dt_exp = jnp.repeat(wl[:, dt_col:dt_col + heads], SSM_HEAD_DIM, axis=1).astype(BF16)
        dt_b_exp = jnp.repeat(dt_bias[l], SSM_HEAD_DIM)[None, :]
        a_log_pad = jnp.pad(a_log[l], (0, LANES - heads))[None, :]
        a_exp = jnp.repeat(-jnp.exp(a_log[l]), SSM_HEAD_DIM)[None, :]
        d_exp = jnp.repeat(d_skip[l], SSM_HEAD_DIM)[None, :]
        n1 = norm1_g[l][None, :]
        n2 = norm2_g[l][None, :]
        ng = ssm_norm_g[l][None, :]
        cw = conv_w[l]
        cb = conv_b[l][None, :]
        lng = sgu_ln_g[l][None, :]
        lnb = sgu_ln_b[l][None, :]
        mix_p = jnp.where(tril, sgu_w[l], 0).astype(BF16)
        bias_p = jnp.repeat(jnp.transpose(sgu_b[l]), SGU_GROUP_DIM, axis=1)
        w_small = jnp.where(tril_s, sgu_w[l][:, :steps, :steps], 0)
        mix_s = jnp.einsum("ab,gts->gatbs", eye, w_small).reshape(ngrp, SGU_CHUNK, SGU_CHUNK).astype(BF16)
        bias_s = jnp.tile(bias_p[:steps], (seqs_per_chunk, 1))
        wa = w_out_a[l].astype(BF16)
        wb = w_out_b[l].astype(BF16)
        wo = w_o[l].astype(BF16)
        wg = w_ffn_gate[l].astype(BF16)
        wu = w_ffn_up[l].astype(BF16)
        wd = w_ffn_down[l].astype(BF16)
        last = l == depth - 1

        su, z, xbc, gt, dt = _in_proj(xp, n1, w_main, w_dt, dt_b, seg_widths, BF16,
                                      _row_tile(xp.shape[0], 512), 1024)
        (a,) = _sgu(su, lng, lnb, mix_p, bias_p, wa, _row_tile(xp.shape[0], 512), False)
        yb, hp, cstp = _ssd_prompt(xbc, z, dt, cw, cb, a_log_pad, d_exp, ng, batch, seq)
        xp = _merge(a, yb, gt, xp, wb, wo, _row_tile(xp.shape[0], 256))
        xp = _ffn(xp, n2, wg, wu, wd, fin_g, _row_tile(xp.shape[0], 512), 512, last)
        ssm_p.append(hp.reshape(batch, heads, SSM_HEAD_DIM, SSM_STATE))
        conv_p.append(cstp)

        su, z, xbc, gt, _ = _in_proj(xs, n1, w_main, w_dt, dt_b, seg_widths, F32,
                                     _row_tile(xs.shape[0], 512), 1024)
        dte = _dt_expand(xs, n1, w_dt_exp, dt_b_exp, 1024)
        a, vrows = _sgu(su, lng, lnb, mix_s, bias_s, wa, _row_tile(xs.shape[0], 512), True)
        yb3, hs, csts = _ssd_sample(
            xbc.reshape(nseq, steps, conv_dim), z.reshape(nseq, steps, d_inner),
            dte.reshape(nseq, steps, d_inner), state_conv,
            state_ssm.reshape(depth, nseq, d_inner, SSM_STATE), l, cw, cb, a_exp, d_exp, ng)
        xs = _merge(a, yb3.reshape(nseq * steps, d_inner), gt, xs, wb, wo, _row_tile(xs.shape[0], 256))
        xs = _ffn(xs, n2, wg, wu, wd, fin_g, _row_tile(xs.shape[0], 512), 512, last)
        ssm_s.append(hs.reshape(nseq, heads, SSM_HEAD_DIM, SSM_STATE))
        conv_s.append(csts)
        v_s.append(vrows.reshape(nseq, steps, sgu_width))

    return (xp.reshape(batch, seq, d), xs.reshape(nseq, steps, d), jnp.stack(ssm_p), jnp.stack(conv_p),
            jnp.stack(ssm_s), jnp.stack(conv_s), jnp.stack(v_s))
```

```python
import functools

import jax
import jax.numpy as jnp
from jax import lax
from jax.experimental import pallas as pl
from jax.experimental.pallas import tpu as pltpu

F32 = jnp.float32
BF16 = jnp.bfloat16
EPS = 1e-6

LANES = 128
SUBLANES = 8
VMEM_CAP_BYTES = 64 * 1024 * 1024

SGU_CHUNK = 128
SGU_GROUP_DIM = 128
SSM_HEAD_DIM = 64
SSM_STATE = 128
SSM_GROUPS = 8
CONV_WIDTH = 4
SSD_CHUNK = 128


def _vmem_limit(estimate_bytes):
    return int(min(estimate_bytes + (8 << 20), VMEM_CAP_BYTES - (6 << 20)))


def _silu(x):
    return x * jax.nn.sigmoid(x)


def _softplus(x):
    return jnp.maximum(x, 0.0) + jnp.log1p(jnp.exp(-jnp.abs(x)))


def _rmsnorm(x, g):
    return x * lax.rsqrt(jnp.mean(x * x, axis=-1, keepdims=True) + EPS) * g


def _dot(a, b):
    return jnp.dot(a, b, preferred_element_type=F32)


def _dot_nt(a, b):
    return lax.dot_general(a, b, (((1,), (1,)), ((), ())), preferred_element_type=F32)


def _dot_tn(a, b):
    return lax.dot_general(a, b, (((0,), (0,)), ((), ())), preferred_element_type=F32)


def _chain(args, in_specs, prev, first_out):
    aliases = {}
    if prev is not None:
        for k, arr in enumerate(prev):
            aliases[len(args)] = first_out + k
            args.append(arr)
            in_specs.append(pl.BlockSpec(memory_space=pl.ANY))
    return aliases


def _in_proj_body(x_ref, g_ref, w_ref, wdt_ref, dtb_ref, su_ref, z_ref, xbc_ref, gt_ref, dt_ref,
                  h_scr, *, n_su, n_z, n_xbc):
    n = pl.program_id(1)

    @pl.when(n == 0)
    def _():
        hb = _rmsnorm(x_ref[...], g_ref[...]).astype(BF16)
        h_scr[...] = hb
        dt_ref[...] = _softplus(_dot(hb, wdt_ref[...]) + dtb_ref[...])

    acc = _dot(h_scr[...], w_ref[...])

    @pl.when(n < n_su)
    def _():
        su_ref[...] = jax.nn.gelu(acc).astype(su_ref.dtype)

    @pl.when((n >= n_su) & (n < n_z))
    def _():
        z_ref[...] = acc.astype(z_ref.dtype)

    @pl.when((n >= n_z) & (n < n_xbc))
    def _():
        xbc_ref[...] = acc.astype(xbc_ref.dtype)

    @pl.when(n >= n_xbc)
    def _():
        gt_ref[...] = jax.nn.sigmoid(acc).astype(gt_ref.dtype)


def _in_proj(x, g, w_main, w_dt, dt_b, seg_widths, act_dtype, tm, tn):
    m, d = x.shape
    w_su, w_z, w_xbc, w_gt = seg_widths
    c_su, c_z, c_xbc, c_gt = w_su // tn, w_z // tn, w_xbc // tn, w_gt // tn
    n_su, n_z, n_xbc = c_su, c_su + c_z, c_su + c_z + c_xbc
    n_tiles = n_xbc + c_gt
    dtw = w_dt.shape[1]

    def seg_map(start, count):
        return lambda i, j: (i, jnp.clip(j - start, 0, count - 1))

    est = 2 * (tm * d * 4 + d * tn * 2 + d * dtw * 2 + 4 * tm * tn * 4 + tm * dtw * 4) + tm * d * 2 + tm * tn * 4
    return pl.pallas_call(
        functools.partial(_in_proj_body, n_su=n_su, n_z=n_z, n_xbc=n_xbc),
        grid=(m // tm, n_tiles),
        in_specs=[
            pl.BlockSpec((tm, d), lambda i, j: (i, 0)),
            pl.BlockSpec((1, d), lambda i, j: (0, 0)),
            pl.BlockSpec((d, tn), lambda i, j: (0, j)),
            pl.BlockSpec((d, dtw), lambda i, j: (0, 0)),
            pl.BlockSpec((1, dtw), lambda i, j: (0, 0)),
        ],
        out_specs=[
            pl.BlockSpec((tm, tn), seg_map(0, c_su)),
            pl.BlockSpec((tm, tn), seg_map(n_su, c_z)),
            pl.BlockSpec((tm, tn), seg_map(n_z, c_xbc)),
            pl.BlockSpec((tm, tn), seg_map(n_xbc, c_gt)),
            pl.BlockSpec((tm, dtw), lambda i, j: (i, 0)),
        ],
        out_shape=[
            jax.ShapeDtypeStruct((m, w_su), act_dtype),
            jax.ShapeDtypeStruct((m, w_z), act_dtype),
            jax.ShapeDtypeStruct((m, w_xbc), act_dtype),
            jax.ShapeDtypeStruct((m, w_gt), act_dtype),
            jax.ShapeDtypeStruct((m, dtw), F32),
        ],
        scratch_shapes=[pltpu.VMEM((tm, d), BF16)],
        compiler_params=pltpu.CompilerParams(
            dimension_semantics=("arbitrary", "arbitrary"), vmem_limit_bytes=_vmem_limit(est)),
        name="in_proj",
    )(x, g, w_main, w_dt, dt_b)


def _dt_expand_body(x_ref, g_ref, w_ref, b_ref, o_ref):
    hb = _rmsnorm(x_ref[...], g_ref[...]).astype(BF16)
    o_ref[...] = _softplus(_dot(hb, w_ref[...]) + b_ref[...])


def _dt_expand(x, g, w_dt_exp, dt_b_exp, tn):
    m, d = x.shape
    n = w_dt_exp.shape[1]
    est = 2 * (m * d * 4 + d * tn * 2 + m * tn * 4) + m * d * 6
    return pl.pallas_call(
        _dt_expand_body,
        grid=(n // tn,),
        in_specs=[
            pl.BlockSpec((m, d), lambda j: (0, 0)),
            pl.BlockSpec((1, d), lambda j: (0, 0)),
            pl.BlockSpec((d, tn), lambda j: (0, j)),
            pl.BlockSpec((1, tn), lambda j: (0, j)),
        ],
        out_specs=pl.BlockSpec((m, tn), lambda j: (0, j)),
        out_shape=jax.ShapeDtypeStruct((m, n), F32),
        compiler_params=pltpu.CompilerParams(
            dimension_semantics=("arbitrary",), vmem_limit_bytes=_vmem_limit(est)),
        name="dt_expand",
    )(x, g, w_dt_exp, dt_b_exp)


def _sgu_body(u_ref, v_ref, lng_ref, lnb_ref, mix_ref, bias_ref, wa_ref, *rest, emit_v, chained):
    if emit_v:
        if chained:
            rest = rest[1:]
        a_ref, vout_ref, ya_scr = rest
    else:
        a_ref, ya_scr = rest
    vg = v_ref[...].astype(F32)
    mu = jnp.mean(vg, axis=-1, keepdims=True)
    var = jnp.mean(jnp.square(vg - mu), axis=-1, keepdims=True)
    v = (vg - mu) * lax.rsqrt(var + EPS) * lng_ref[...] + lnb_ref[...]
    if emit_v:
        vout_ref[...] = v
    vb = v.astype(BF16)
    rows, width = vb.shape
    for c in range(rows // SGU_CHUNK):
        rs = slice(c * SGU_CHUNK, (c + 1) * SGU_CHUNK)
        for grp in range(width // SGU_GROUP_DIM):
            cs = slice(grp * SGU_GROUP_DIM, (grp + 1) * SGU_GROUP_DIM)
            mixed = _dot(mix_ref[grp], vb[rs, cs]) + bias_ref[:, cs]
            ya_scr[rs, cs] = (u_ref[rs, cs].astype(F32) * mixed).astype(BF16)
    a_ref[...] = _dot(ya_scr[...], wa_ref[...]).astype(a_ref.dtype)


def _sgu(su, lng, lnb, mix, bias, w_out_a, rows, emit_v, depth=None, layer=None, prev=None):
    m = su.shape[0]
    width = su.shape[1] // 2
    d = w_out_a.shape[1]
    ngrp = mix.shape[0]
    isz = su.dtype.itemsize
    est = 2 * (2 * rows * width * isz + ngrp * SGU_CHUNK * SGU_CHUNK * 2 + SGU_CHUNK * width * 4
               + rows * d * 2 + rows * width * 4) + width * d * 2 + 4 * rows * width * 4
    out_shape = [jax.ShapeDtypeStruct((m, d), BF16)]
    out_specs = [pl.BlockSpec((rows, d), lambda i: (i, 0))]
    if emit_v:
        out_shape.append(jax.ShapeDtypeStruct((depth, m, width), F32))
        out_specs.append(pl.BlockSpec((None, rows, width), lambda i: (layer, i, 0)))
    in_specs = [
        pl.BlockSpec((rows, width), lambda i: (i, 0)),
        pl.BlockSpec((rows, width), lambda i: (i, 1)),
        pl.BlockSpec((1, width), lambda i: (0, 0)),
        pl.BlockSpec((1, width), lambda i: (0, 0)),
        pl.BlockSpec((ngrp, SGU_CHUNK, SGU_CHUNK), lambda i: (0, 0, 0)),
        pl.BlockSpec((SGU_CHUNK, width), lambda i: (0, 0)),
        pl.BlockSpec((width, d), lambda i: (0, 0), pipeline_mode=pl.Buffered(1)),
    ]
    args = [su, su, lng, lnb, mix, bias, w_out_a]
    aliases = _chain(args, in_specs, prev, 1)
    return pl.pallas_call(
        functools.partial(_sgu_body, emit_v=emit_v, chained=prev is not None),
        grid=(m // rows,),
        in_specs=in_specs,
        out_specs=out_specs,
        out_shape=out_shape,
        scratch_shapes=[pltpu.VMEM((rows, width), BF16)],
        input_output_aliases=aliases,
        compiler_params=pltpu.CompilerParams(
            dimension_semantics=("arbitrary",), vmem_limit_bytes=_vmem_limit(est)),
        name="sgu",
    )(*args)


def _prefix_sum_rows(x, row):
    n = x.shape[0]
    s = 1
    while s < n:
        x = x + jnp.where(row >= s, pltpu.roll(x, s, 0), 0.0)
        s *= 2
    return x


def _gate_groupnorm(y, z, ng):
    t = y * _silu(z)
    return t * lax.rsqrt(jnp.mean(t * t, axis=-1, keepdims=True) + EPS) * ng


def _split_hi_lo(v):
    hi = v.astype(BF16)
    lo = (v - hi.astype(F32)).astype(BF16)
    return jnp.concatenate([hi, lo], axis=1)


def _ssd_prompt_body(xbc_ref, z_ref, dt_ref, cw_ref, cb_ref, alog_ref, dexp_ref, ng_ref, e2_ref, *rest,
                     d_inner, n_chunks, chained):
    if chained:
        rest = rest[2:]
    yb_ref, h_ref, cst_ref, win_scr, act_scr, y_scr, ht_scr, cst_scr = rest
    c = pl.program_id(1)
    q = xbc_ref.shape[0]
    conv_dim = xbc_ref.shape[1]
    p = SSM_HEAD_DIM
    n_state = SSM_STATE
    gw = d_inner // SSM_GROUPS
    pairs_per_group = gw // (2 * p)
    halo = SUBLANES

    @pl.when(c == 0)
    def _():
        win_scr[0:halo, :] = jnp.zeros((halo, conv_dim), F32)
        ht_scr[...] = jnp.zeros(ht_scr.shape, F32)

    win_scr[halo:halo + q, :] = xbc_ref[...].astype(F32)
    lane_chunk = 512
    for j in range(conv_dim // lane_chunk):
        ls = slice(j * lane_chunk, (j + 1) * lane_chunk)
        acc = cb_ref[:, ls] + cw_ref[0:1, ls] * win_scr[halo - 3:halo - 3 + q, ls]
        for k in range(1, CONV_WIDTH):
            acc = acc + cw_ref[k:k + 1, ls] * win_scr[halo - 3 + k:halo - 3 + k + q, ls]
        act_scr[:, ls] = _silu(acc)

    @pl.when(c == n_chunks - 1)
    def _():
        cst_ref[...] = win_scr[halo + q - 3:halo + q, :]

    win_scr[0:halo, :] = win_scr[q:q + halo, :]

    row = lax.broadcasted_iota(jnp.int32, (q, LANES), 0)
    lane = lax.broadcasted_iota(jnp.int32, (q, LANES), 1)
    dt = dt_ref[...]
    da = dt * (-jnp.exp(alog_ref[...]))
    cs = _prefix_sum_rows(da, row)
    cst_scr[...] = cs.T
    dt_hl = _split_hi_lo(dt)
    ecs_hl = _split_hi_lo(jnp.exp(cs))
    dec_hl = _split_hi_lo(jnp.exp(cs[q - 1:q, :] - cs))
    causal = row >= lane
    lower_half = lane < p

    for grp in range(SSM_GROUPS):
        gs = slice(grp * gw, (grp + 1) * gw)
        e2 = e2_ref[:, gs]
        dt_g = _dot(dt_hl, e2)
        ecs_g = _dot(ecs_hl, e2)
        dec_g = _dot(dec_hl, e2)
        xs_g = act_scr[:, gs]
        xdt_g = xs_g * dt_g
        b_g = act_scr[:, d_inner + grp * n_state:d_inner + (grp + 1) * n_state].astype(BF16)
        c_g = act_scr[:, d_inner + (SSM_GROUPS + grp) * n_state:
                      d_inner + (SSM_GROUPS + grp + 1) * n_state].astype(BF16)
        cbm = jnp.where(causal, _dot_nt(c_g, b_g), 0.0)
        ht_g = ht_scr[:, gs]
        y_g = _dot(c_g, ht_g.astype(BF16)) * ecs_g + dexp_ref[:, gs] * xs_g
        for j in range(pairs_per_group):
            h0 = grp * 2 * pairs_per_group + 2 * j
            ws = []
            for h in (h0, h0 + 1):
                seg = cs[:, h:h + 1] - cst_scr[h:h + 1, :]
                ws.append((cbm * jnp.exp(jnp.where(causal, seg, 0.0))).astype(BF16))
            xpair = xdt_g[:, j * LANES:(j + 1) * LANES]
            rhs = jnp.concatenate([jnp.where(lower_half, xpair, 0.0).astype(BF16),
                                   jnp.where(lower_half, 0.0, xpair).astype(BF16)], axis=0)
            ls = slice(grp * gw + j * LANES, grp * gw + (j + 1) * LANES)
            y_scr[:, ls] = y_g[:, j * LANES:(j + 1) * LANES] + _dot(jnp.concatenate(ws, axis=1), rhs)
        states_t = _dot_tn(b_g, (xdt_g * dec_g).astype(BF16))
        ht_scr[:, gs] = ht_g * ecs_g[q - 1:q, :] + states_t

    for grp in range(SSM_GROUPS):
        gs = slice(grp * gw, (grp + 1) * gw)
        yb_ref[:, gs] = _gate_groupnorm(y_scr[:, gs], z_ref[:, gs].astype(F32), ng_ref[:, gs]).astype(yb_ref.dtype)

    @pl.when(c == n_chunks - 1)
    def _():
        for grp in range(SSM_GROUPS):
            gs = slice(grp * gw, (grp + 1) * gw)
            h_ref[gs, :] = ht_scr[:, gs].T


def _ssd_prompt(xbc, z, dt, conv_w, conv_b, a_log, d_exp, norm_g, e2, batch, seq, depth, layer, prev):
    q = SSD_CHUNK
    n_chunks = seq // q
    conv_dim = xbc.shape[1]
    d_inner = z.shape[1]
    dtw = dt.shape[1]
    est = (2 * (q * conv_dim * 2 + q * d_inner * 2 + q * dtw * 4 + q * d_inner * 2 + d_inner * SSM_STATE * 4
                + 8 * conv_dim * 4 + 5 * conv_dim * 4 + 2 * d_inner * 4 + 2 * LANES * d_inner * 2)
           + (q + 16) * conv_dim * 4 + q * conv_dim * 4 + 2 * q * d_inner * 4 + (8 << 20))
    row_map = lambda b, c: (b * n_chunks + c, 0)
    const = lambda b, c: (0, 0)
    in_specs = [
        pl.BlockSpec((q, conv_dim), row_map),
        pl.BlockSpec((q, d_inner), row_map),
        pl.BlockSpec((q, dtw), row_map),
        pl.BlockSpec((CONV_WIDTH, conv_dim), const),
        pl.BlockSpec((1, conv_dim), const),
        pl.BlockSpec((1, dtw), const),
        pl.BlockSpec((1, d_inner), const),
        pl.BlockSpec((1, d_inner), const),
        pl.BlockSpec((2 * LANES, d_inner), const),
    ]
    args = [xbc, z, dt, conv_w, conv_b, a_log, d_exp, norm_g, e2]
    aliases = _chain(args, in_specs, prev, 1)
    return pl.pallas_call(
        functools.partial(_ssd_prompt_body, d_inner=d_inner, n_chunks=n_chunks, chained=prev is not None),
        grid=(batch, n_chunks),
        in_specs=in_specs,
        out_specs=[
            pl.BlockSpec((q, d_inner), row_map),
            pl.BlockSpec((None, None, d_inner, SSM_STATE), lambda b, c: (layer, b, 0, 0)),
            pl.BlockSpec((None, None, CONV_WIDTH - 1, conv_dim), lambda b, c: (layer, b, 0, 0)),
        ],
        out_shape=[
            jax.ShapeDtypeStruct((batch * seq, d_inner), BF16),
            jax.ShapeDtypeStruct((depth, batch, d_inner, SSM_STATE), F32),
            jax.ShapeDtypeStruct((depth, batch, CONV_WIDTH - 1, conv_dim), F32),
        ],
        scratch_shapes=[
            pltpu.VMEM((q + 2 * SUBLANES, conv_dim), F32),
            pltpu.VMEM((q, conv_dim), F32),
            pltpu.VMEM((q, d_inner), F32),
            pltpu.VMEM((SSM_STATE, d_inner), F32),
            pltpu.VMEM((LANES, q), F32),
        ],
        input_output_aliases=aliases,
        compiler_params=pltpu.CompilerParams(
            dimension_semantics=("arbitrary", "arbitrary"), vmem_limit_bytes=_vmem_limit(est)),
        name="ssd_prompt",
    )(*args)


def _ssd_sample_body(xbc_ref, z_ref, dte_ref, cbuf_ref, h0_ref, cw_ref, cb_ref, aexp_ref, dexp_ref, ng_ref,
                     *rest, d_inner, chained):
    if chained:
        rest = rest[2:]
    yb_ref, h_ref, cst_ref, win_scr, pad_scr = rest
    steps = xbc_ref.shape[0]
    conv_dim = xbc_ref.shape[1]
    q = SUBLANES
    n_state = SSM_STATE
    gw = d_inner // SSM_GROUPS
    halo = SUBLANES

    win_scr[...] = jnp.zeros(win_scr.shape, F32)
    win_scr[halo - 3:halo, :] = cbuf_ref[...]
    win_scr[halo:halo + steps, :] = xbc_ref[...].astype(F32)
    acc = cb_ref[...] + cw_ref[0:1, :] * win_scr[halo - 3:halo - 3 + q, :]
    for k in range(1, CONV_WIDTH):
        acc = acc + cw_ref[k:k + 1, :] * win_scr[halo - 3 + k:halo - 3 + k + q, :]
    act = _silu(acc)
    cst_ref[...] = win_scr[halo + steps - 3:halo + steps, :]

    xs = act[:, :d_inner]
    row = lax.broadcasted_iota(jnp.int32, (q, d_inner), 0)
    pad_scr[...] = jnp.zeros(pad_scr.shape, F32)
    pad_scr[0:steps, :] = dte_ref[...]
    dt = pad_scr[...]
    da = dt * aexp_ref[...]
    cs = _prefix_sum_rows(da, row)
    cs_last = cs[steps - 1:steps, :]
    xdt = xs * dt

    y = dexp_ref[...] * xs
    for s in range(steps):
        cb_parts = []
        for grp in range(SSM_GROUPS):
            b_row = act[s:s + 1, d_inner + grp * n_state:d_inner + (grp + 1) * n_state]
            c_g = act[:, d_inner + (SSM_GROUPS + grp) * n_state:d_inner + (SSM_GROUPS + grp + 1) * n_state]
            cb_parts.append(jnp.broadcast_to(jnp.sum(c_g * b_row, axis=1, keepdims=True), (q, gw)))
        cb_s = jnp.concatenate(cb_parts, axis=1)
        keep = row >= s
        lmat = jnp.where(keep, jnp.exp(jnp.where(keep, cs - cs[s:s + 1, :], 0.0)), 0.0)
        y = y + cb_s * lmat * xdt[s:s + 1, :]

    ecs = jnp.exp(cs)
    xs_dec = (xdt * jnp.exp(cs_last - cs)).astype(BF16)
    cd = jnp.exp(cs_last)
    cd_col = jnp.broadcast_to(cd, (q, d_inner)).T
    for grp in range(SSM_GROUPS):
        gs = slice(grp * gw, (grp + 1) * gw)
        b_g = act[:, d_inner + grp * n_state:d_inner + (grp + 1) * n_state].astype(BF16)
        c_g = act[:, d_inner + (SSM_GROUPS + grp) * n_state:
                  d_inner + (SSM_GROUPS + grp + 1) * n_state].astype(BF16)
        h_g = h0_ref[gs, :]
        y_off = _dot_nt(c_g, h_g.astype(BF16))
        y_g = y[:, gs] + y_off * ecs[:, gs]
        t = _gate_groupnorm(y_g[0:steps, :], z_ref[:, gs].astype(F32), ng_ref[:, gs])
        yb_ref[:, gs] = t.astype(yb_ref.dtype)
        states = _dot_tn(xs_dec[:, gs], b_g)
        h_ref[gs, :] = h_g * cd_col[gs, 0:1] + states


def _ssd_sample(xbc3, z3, dte3, conv_state, h0_all, layer, conv_w, conv_b, a_exp, d_exp, norm_g, prev):
    nseq, steps, conv_dim = xbc3.shape
    d_inner = z3.shape[2]
    depth = h0_all.shape[0]
    q = SUBLANES
    est = (2 * (2 * d_inner * SSM_STATE * 4 + steps * (conv_dim + 2 * d_inner) * 4 + 3 * conv_dim * 4
                + steps * d_inner * 4 + 3 * conv_dim * 4 + 5 * conv_dim * 4 + 3 * d_inner * 4)
           + 3 * q * conv_dim * 4 + d_inner * LANES * 4 + (12 << 20))
    seq_map = lambda s: (s, 0, 0)
    lay_map = lambda s: (layer, s, 0, 0)
    const = lambda s: (0, 0)
    in_specs = [
        pl.BlockSpec((None, steps, conv_dim), seq_map),
        pl.BlockSpec((None, steps, d_inner), seq_map),
        pl.BlockSpec((None, steps, d_inner), seq_map),
        pl.BlockSpec((None, None, CONV_WIDTH - 1, conv_dim), lay_map),
        pl.BlockSpec((None, None, d_inner, SSM_STATE), lay_map),
        pl.BlockSpec((CONV_WIDTH, conv_dim), const),
        pl.BlockSpec((1, conv_dim), const),
        pl.BlockSpec((1, d_inner), const),
        pl.BlockSpec((1, d_inner), const),
        pl.BlockSpec((1, d_inner), const),
    ]
    args = [xbc3, z3, dte3, conv_state, h0_all, conv_w, conv_b, a_exp, d_exp, norm_g]
    aliases = _chain(args, in_specs, prev, 1)
    return pl.pallas_call(
        functools.partial(_ssd_sample_body, d_inner=d_inner, chained=prev is not None),
        grid=(nseq,),
        in_specs=in_specs,
        out_specs=[
            pl.BlockSpec((None, steps, d_inner), seq_map),
            pl.BlockSpec((None, None, d_inner, SSM_STATE), lay_map),
            pl.BlockSpec((None, None, CONV_WIDTH - 1, conv_dim), lay_map),
        ],
        out_shape=[
            jax.ShapeDtypeStruct((nseq, steps, d_inner), F32),
            jax.ShapeDtypeStruct((depth, nseq, d_inner, SSM_STATE), F32),
            jax.ShapeDtypeStruct((depth, nseq, CONV_WIDTH - 1, conv_dim), F32),
        ],
        scratch_shapes=[
            pltpu.VMEM((3 * SUBLANES, conv_dim), F32),
            pltpu.VMEM((q, d_inner), F32),
        ],
        input_output_aliases=aliases,
        compiler_params=pltpu.CompilerParams(
            dimension_semantics=("arbitrary",), vmem_limit_bytes=_vmem_limit(est)),
        name="ssd_sample",
    )(*args)


def _merge_body(a_ref, yb_ref, ga_ref, gb_ref, x_ref, wb_ref, wo_ref, o_ref):
    b = _dot(yb_ref[...].astype(BF16), wb_ref[...])
    merged = ga_ref[...].astype(F32) * a_ref[...].astype(F32) + gb_ref[...].astype(F32) * b
    o_ref[...] = x_ref[...] + _dot(merged.astype(BF16), wo_ref[...])


def _merge(a, yb, gates, x, w_out_b, w_o, rows):
    m, d = x.shape
    d_inner = yb.shape[1]
    est = (2 * rows * (d * 2 + d_inner * yb.dtype.itemsize + 2 * d * gates.dtype.itemsize + 2 * d * 4)
           + d_inner * d * 2 + d * d * 2 + 3 * rows * d * 4)
    return pl.pallas_call(
        _merge_body,
        grid=(m // rows,),
        in_specs=[
            pl.BlockSpec((rows, d), lambda i: (i, 0)),
            pl.BlockSpec((rows, d_inner), lambda i: (i, 0)),
            pl.BlockSpec((rows, d), lambda i: (i, 0)),
            pl.BlockSpec((rows, d), lambda i: (i, 1)),
            pl.BlockSpec((rows, d), lambda i: (i, 0)),
            pl.BlockSpec((d_inner, d), lambda i: (0, 0), pipeline_mode=pl.Buffered(1)),
            pl.BlockSpec((d, d), lambda i: (0, 0), pipeline_mode=pl.Buffered(1)),
        ],
        out_specs=pl.BlockSpec((rows, d), lambda i: (i, 0)),
        out_shape=jax.ShapeDtypeStruct((m, d), F32),
        compiler_params=pltpu.CompilerParams(
            dimension_semantics=("arbitrary",), vmem_limit_bytes=_vmem_limit(est)),
        name="merge",
    )(a, yb, gates, gates, x, w_out_b, w_o)


def _ffn_body(x_ref, g_ref, wg_ref, wu_ref, wd_ref, fg_ref, o_ref, h_scr, acc_scr, *, n_f, final_norm):
    f = pl.program_id(1)

    @pl.when(f == 0)
    def _():
        h_scr[...] = _rmsnorm(x_ref[...], g_ref[...]).astype(BF16)
        acc_scr[...] = jnp.zeros(acc_scr.shape, F32)

    hb = h_scr[...]
    act = _silu(_dot(hb, wg_ref[...])) * _dot(hb, wu_ref[...])
    acc_scr[...] += _dot(act.astype(BF16), wd_ref[...])

    @pl.when(f == n_f - 1)
    def _():
        y = x_ref[...] + acc_scr[...]
        if final_norm:
            y = _rmsnorm(y, fg_ref[...])
        o_ref[...] = y


def _ffn(x, g, w_gate, w_up, w_down, final_g, tm, tf, final_norm):
    m, d = x.shape
    hidden = w_gate.shape[1]
    n_f = hidden // tf
    est = 2 * (2 * tm * d * 4 + 3 * d * tf * 2) + tm * d * 6 + 3 * tm * tf * 4
    return pl.pallas_call(
        functools.partial(_ffn_body, n_f=n_f, final_norm=final_norm),
        grid=(m // tm, n_f),
        in_specs=[
            pl.BlockSpec((tm, d), lambda i, j: (i, 0)),
            pl.BlockSpec((1, d), lambda i, j: (0, 0)),
            pl.BlockSpec((d, tf), lambda i, j: (0, j)),
            pl.BlockSpec((d, tf), lambda i, j: (0, j)),
            pl.BlockSpec((tf, d), lambda i, j: (j, 0)),
            pl.BlockSpec((1, d), lambda i, j: (0, 0)),
        ],
        out_specs=pl.BlockSpec((tm, d), lambda i, j: (i, 0)),
        out_shape=jax.ShapeDtypeStruct((m, d), F32),
        scratch_shapes=[pltpu.VMEM((tm, d), BF16), pltpu.VMEM((tm, d), F32)],
        compiler_params=pltpu.CompilerParams(
            dimension_semantics=("arbitrary", "arbitrary"), vmem_limit_bytes=_vmem_limit(est)),
        name="ffn",
    )(x, g, w_gate, w_up, w_down, final_g)


def _row_tile(m, target):
    t = min(m, target)
    while m % t:
        t //= 2
    return t


def kernel(x_prompt, x_sample, state_ssm, state_conv, norm1_g, w_in, conv_w, conv_b, dt_bias, a_log, d_skip, ssm_norm_g, sgu_ln_g, sgu_ln_b, sgu_w, sgu_b, w_out_a, w_out_b, w_o, norm2_g, w_ffn_gate, w_ffn_up, w_ffn_down, final_norm_g):
    batch, seq, d = x_prompt.shape
    nseq, steps, _ = x_sample.shape
    depth = w_in.shape[0]
    heads = dt_bias.shape[1]
    d_inner = heads * SSM_HEAD_DIM
    conv_dim = conv_w.shape[2]
    sgu_width = sgu_ln_g.shape[1]
    ngrp = sgu_w.shape[1]
    seg_widths = (2 * sgu_width, d_inner, conv_dim, 2 * d)
    dt_col = 2 * sgu_width + d_inner + conv_dim

    xp = x_prompt.reshape(batch * seq, d)
    xs = x_sample.reshape(nseq * steps, d)
    fin_g = final_norm_g[None, :]
    state_ssm_flat = state_ssm.reshape(depth, nseq, d_inner, SSM_STATE)

    tril = jnp.tril(jnp.ones((SGU_CHUNK, SGU_CHUNK), bool))
    seqs_per_chunk = SGU_CHUNK // steps
    eye = jnp.eye(seqs_per_chunk, dtype=F32)
    tril_s = jnp.tril(jnp.ones((steps, steps), bool))
    expand = (jnp.arange(d_inner)[None, :] // SSM_HEAD_DIM == jnp.arange(LANES)[:, None]).astype(BF16)
    e2 = jnp.concatenate([expand, expand], axis=0)

    prev_p = prev_s = prev_v = None
    for l in range(depth):
        wl = w_in[l]
        w_main = jnp.concatenate([wl[:, :dt_col], wl[:, dt_col + heads:]], axis=1).astype(BF16)
        w_dt = jnp.pad(wl[:, dt_col:dt_col + heads], ((0, 0), (0, LANES - heads))).astype(BF16)
        dt_b = jnp.pad(dt_bias[l], (0, LANES - heads))[None, :]
        w_dt_exp = jnp.repeat(wl[:, dt_col:dt_col + heads], SSM_HEAD_DIM, axis=1).astype(BF16)
        dt_b_exp = jnp.repeat(dt_bias[l], SSM_HEAD_DIM)[None, :]
        a_log_pad = jnp.pad(a_log[l], (0, LANES - heads))[None, :]
        a_exp = jnp.repeat(-jnp.exp(a_log[l]), SSM_HEAD_DIM)[None, :]
        d_exp = jnp.repeat(d_skip[l], SSM_HEAD_DIM)[None, :]
        n1 = norm1_g[l][None, :]
        n2 = norm2_g[l][None, :]
        ng = ssm_norm_g[l][None, :]
        cw = conv_w[l]
        cb = conv_b[l][None, :]
        lng = sgu_ln_g[l][None, :]
        lnb = sgu_ln_b[l][None, :]
        mix_p = jnp.where(tril, sgu_w[l], 0).astype(BF16)
        bias_p = jnp.repeat(jnp.transpose(sgu_b[l]), SGU_GROUP_DIM, axis=1)
        w_small = jnp.where(tril_s, sgu_w[l][:, :steps, :steps], 0)
        mix_s = jnp.einsum("ab,gts->gatbs", eye, w_small).reshape(ngrp, SGU_CHUNK, SGU_CHUNK).astype(BF16)
        bias_s = jnp.tile(bias_p[:steps], (seqs_per_chunk, 1))
        wa = w_out_a[l].astype(BF16)
        wb = w_out_b[l].astype(BF16)
        wo = w_o[l].astype(BF16)
        wg = w_ffn_gate[l].astype(BF16)
        wu = w_ffn_up[l].astype(BF16)
        wd = w_ffn_down[l].astype(BF16)
        last = l == depth - 1

        su, z, xbc, gt, dt = _in_proj(xp, n1, w_main, w_dt, dt_b, seg_widths, BF16,
                                      _row_tile(xp.shape[0], 512), 1024)
        (a,) = _sgu(su, lng, lnb, mix_p, bias_p, wa, _row_tile(xp.shape[0], 512), False)
        yb, *prev_p = _ssd_prompt(xbc, z, dt, cw, cb, a_log_pad, d_exp, ng, e2, batch, seq, depth, l, prev_p)
        xp = _merge(a, yb, gt, xp, wb, wo, _row_tile(xp.shape[0], 256))
        xp = _ffn(xp, n2, wg, wu, wd, fin_g, _row_tile(xp.shape[0], 512), 512, last)

        su, z, xbc, gt, _ = _in_proj(xs, n1, w_main, w_dt, dt_b, seg_widths, F32,
                                     _row_tile(xs.shape[0], 512), 1024)
        dte = _dt_expand(xs, n1, w_dt_exp, dt_b_exp, 1024)
        a, *prev_v = _sgu(su, lng, lnb, mix_s, bias_s, wa, _row_tile(xs.shape[0], 512), True, depth, l, prev_v)
        yb3, *prev_s = _ssd_sample(
            xbc.reshape(nseq, steps, conv_dim), z.reshape(nseq, steps, d_inner),
            dte.reshape(nseq, steps, d_inner), state_conv, state_ssm_flat, l, cw, cb, a_exp, d_exp, ng, prev_s)
        xs = _merge(a, yb3.reshape(nseq * steps, d_inner), gt, xs, wb, wo, _row_tile(xs.shape[0], 256))
        xs = _ffn(xs, n2, wg, wu, wd, fin_g, _row_tile(xs.shape[0], 512), 512, last)

    ssm_p, conv_p = prev_p
    ssm_s, conv_s = prev_s
    (v_s,) = prev_v
    return (xp.reshape(batch, seq, d), xs.reshape(nseq, steps, d),
            ssm_p.reshape(depth, batch, heads, SSM_HEAD_DIM, SSM_STATE), conv_p,
            ssm_s.reshape(depth, nseq, heads, SSM_HEAD_DIM, SSM_STATE), conv_s,
            v_s.reshape(depth, nseq, steps, sgu_width))
```

```python
import functools

import jax
import jax.numpy as jnp
from jax import lax
from jax.experimental import pallas as pl
from jax.experimental.pallas import tpu as pltpu

F32 = jnp.float32
BF16 = jnp.bfloat16
EPS = 1e-6

LANES = 128
SUBLANES = 8
VMEM_CAP_BYTES = 64 * 1024 * 1024

SGU_CHUNK = 128
SGU_GROUP_DIM = 128
SSM_HEAD_DIM = 64
SSM_STATE = 128
SSM_GROUPS = 8
CONV_WIDTH = 4
SSD_CHUNK = 128


def _vmem_limit(estimate_bytes):
    return int(min(estimate_bytes + (8 << 20), VMEM_CAP_BYTES - (6 << 20)))


def _silu(x):
    return x * jax.nn.sigmoid(x)


def _softplus(x):
    return jnp.maximum(x, 0.0) + jnp.log1p(jnp.exp(-jnp.abs(x)))


def _rmsnorm(x, g):
    return x * lax.rsqrt(jnp.mean(x * x, axis=-1, keepdims=True) + EPS) * g


def _dot(a, b):
    return jnp.dot(a, b, preferred_element_type=F32)


def _dot_nt(a, b):
    return lax.dot_general(a, b, (((1,), (1,)), ((), ())), preferred_element_type=F32)


def _dot_tn(a, b):
    return lax.dot_general(a, b, (((0,), (0,)), ((), ())), preferred_element_type=F32)


def _chain(args, in_specs, prev, first_out):
    aliases = {}
    if prev is not None:
        for k, arr in enumerate(prev):
            aliases[len(args)] = first_out + k
            args.append(arr)
            in_specs.append(pl.BlockSpec(memory_space=pl.ANY))
    return aliases


def _row_blocks(rows, sub):
    sub = min(rows, sub)
    return [slice(r * sub, (r + 1) * sub) for r in range(rows // sub)]


def _lspec(block, layer, tail_index):
    return pl.BlockSpec((None,) + tuple(block), lambda *ids: (layer,) + tuple(tail_index(*ids)))


def _in_proj_body(x_ref, g_ref, w_ref, wgt_ref, wdt_ref, dtb_ref, su_ref, z_ref, xbc_ref, gt_ref, dt_ref,
                  h_scr, *, n_su, n_z, n_xbc, sub):
    n = pl.program_id(1)
    blocks = _row_blocks(x_ref.shape[0], sub)

    @pl.when(n == 0)
    def _():
        for rs in blocks:
            hb = _rmsnorm(x_ref[rs, :], g_ref[...]).astype(BF16)
            h_scr[rs, :] = hb
            dt_ref[rs, :] = _softplus(_dot(hb, wdt_ref[...]) + dtb_ref[...])

    def arm(o_ref, weight_ref, act):
        def run():
            for rs in blocks:
                o_ref[rs, :] = act(_dot(h_scr[rs, :], weight_ref[...])).astype(o_ref.dtype)
        return run

    pl.when(n < n_su)(arm(su_ref, w_ref, jax.nn.gelu))
    pl.when((n >= n_su) & (n < n_z))(arm(z_ref, w_ref, lambda t: t))
    pl.when((n >= n_z) & (n < n_xbc))(arm(xbc_ref, w_ref, lambda t: t))
    pl.when(n >= n_xbc)(arm(gt_ref, wgt_ref, jax.nn.sigmoid))


def _in_proj(x, g, w_in_b, w_gates_b, dt_b, layer, seg_widths, act_dtype, tm, tn):
    m, d = x.shape
    w_su, w_z, w_xbc, w_gt = seg_widths
    c_su, c_z, c_xbc, c_gt = w_su // tn, w_z // tn, w_xbc // tn, w_gt // tn
    n_su, n_z, n_xbc = c_su, c_su + c_z, c_su + c_z + c_xbc
    n_tiles = n_xbc + c_gt
    dtw = LANES
    dt_block = (w_su + w_z + w_xbc) // dtw

    def seg_map(start, count):
        return lambda i, j: (i, jnp.clip(j - start, 0, count - 1))

    osz = jnp.dtype(act_dtype).itemsize
    est = (tm * d * 4 + 2 * (2 * d * tn * 2 + d * dtw * 2 + 4 * tm * tn * osz + tm * dtw * 4)
           + tm * d * 2 + 256 * tn * 8)
    return pl.pallas_call(
        functools.partial(_in_proj_body, n_su=n_su, n_z=n_z, n_xbc=n_xbc, sub=256),
        grid=(m // tm, n_tiles),
        in_specs=[
            pl.BlockSpec((tm, d), lambda i, j: (i, 0), pipeline_mode=pl.Buffered(1)),
            _lspec((1, d), layer, lambda i, j: (0, 0)),
            _lspec((d, tn), layer, lambda i, j: (0, jnp.minimum(j, n_xbc - 1))),
            _lspec((d, tn), layer, lambda i, j: (0, jnp.maximum(j - n_xbc, 0))),
            _lspec((d, dtw), layer, lambda i, j: (0, dt_block)),
            _lspec((1, dtw), layer, lambda i, j: (0, 0)),
        ],
        out_specs=[
            pl.BlockSpec((tm, tn), seg_map(0, c_su)),
            pl.BlockSpec((tm, tn), seg_map(n_su, c_z)),
            pl.BlockSpec((tm, tn), seg_map(n_z, c_xbc)),
            pl.BlockSpec((tm, tn), seg_map(n_xbc, c_gt)),
            pl.BlockSpec((tm, dtw), lambda i, j: (i, 0)),
        ],
        out_shape=[
            jax.ShapeDtypeStruct((m, w_su), act_dtype),
            jax.ShapeDtypeStruct((m, w_z), act_dtype),
            jax.ShapeDtypeStruct((m, w_xbc), act_dtype),
            jax.ShapeDtypeStruct((m, w_gt), act_dtype),
            jax.ShapeDtypeStruct((m, dtw), F32),
        ],
        scratch_shapes=[pltpu.VMEM((tm, d), BF16)],
        compiler_params=pltpu.CompilerParams(
            dimension_semantics=("arbitrary", "arbitrary"), vmem_limit_bytes=_vmem_limit(est)),
        name="in_proj",
    )(x, g, w_in_b, w_gates_b, w_in_b, dt_b)


def _dt_expand_body(x_ref, g_ref, w_ref, b_ref, o_ref):
    hb = _rmsnorm(x_ref[...], g_ref[...]).astype(BF16)
    o_ref[...] = _softplus(_dot(hb, w_ref[...]) + b_ref[...])


def _dt_expand(x, g, w_dt_exp, dt_b_exp, layer, tn):
    m, d = x.shape
    n = w_dt_exp.shape[2]
    est = 2 * (m * d * 4 + d * tn * 2 + m * tn * 4) + m * d * 6
    return pl.pallas_call(
        _dt_expand_body,
        grid=(n // tn,),
        in_specs=[
            pl.BlockSpec((m, d), lambda j: (0, 0)),
            _lspec((1, d), layer, lambda j: (0, 0)),
            _lspec((d, tn), layer, lambda j: (0, j)),
            _lspec((1, tn), layer, lambda j: (0, j)),
        ],
        out_specs=pl.BlockSpec((m, tn), lambda j: (0, j)),
        out_shape=jax.ShapeDtypeStruct((m, n), F32),
        compiler_params=pltpu.CompilerParams(
            dimension_semantics=("arbitrary",), vmem_limit_bytes=_vmem_limit(est)),
        name="dt_expand",
    )(x, g, w_dt_exp, dt_b_exp)


def _sgu_body(u_ref, v_ref, lng_ref, lnb_ref, mix_ref, bias_ref, wa_ref, *rest, emit_v, chained):
    if emit_v:
        if chained:
            rest = rest[1:]
        a_ref, vout_ref, ya_scr = rest
    else:
        a_ref, ya_scr = rest
    vg = v_ref[...].astype(F32)
    mu = jnp.mean(vg, axis=-1, keepdims=True)
    var = jnp.mean(jnp.square(vg - mu), axis=-1, keepdims=True)
    v = (vg - mu) * lax.rsqrt(var + EPS) * lng_ref[...] + lnb_ref[...]
    if emit_v:
        vout_ref[...] = v
    vb = v.astype(BF16)
    rows, width = vb.shape
    for c in range(rows // SGU_CHUNK):
        rs = slice(c * SGU_CHUNK, (c + 1) * SGU_CHUNK)
        for grp in range(width // SGU_GROUP_DIM):
            cs = slice(grp * SGU_GROUP_DIM, (grp + 1) * SGU_GROUP_DIM)
            mixed = _dot(mix_ref[grp], vb[rs, cs]) + bias_ref[:, cs]
            ya_scr[rs, cs] = (u_ref[rs, cs].astype(F32) * mixed).astype(BF16)
    a_ref[...] = _dot(ya_scr[...], wa_ref[...]).astype(a_ref.dtype)


def _sgu(su, lng, lnb, mix, bias, w_out_a, layer, rows, emit_v, prev=None):
    m = su.shape[0]
    width = su.shape[1] // 2
    depth, _, d = w_out_a.shape
    ngrp = mix.shape[1]
    isz = su.dtype.itemsize
    est = 2 * (2 * rows * width * isz + ngrp * SGU_CHUNK * SGU_CHUNK * 2 + SGU_CHUNK * width * 4
               + rows * d * 2 + rows * width * 4) + width * d * 2 + 4 * rows * width * 4
    out_shape = [jax.ShapeDtypeStruct((m, d), BF16)]
    out_specs = [pl.BlockSpec((rows, d), lambda i: (i, 0))]
    if emit_v:
        out_shape.append(jax.ShapeDtypeStruct((depth, m, width), F32))
        out_specs.append(pl.BlockSpec((None, rows, width), lambda i: (layer, i, 0)))
    in_specs = [
        pl.BlockSpec((rows, width), lambda i: (i, 0)),
        pl.BlockSpec((rows, width), lambda i: (i, 1)),
        _lspec((1, width), layer, lambda i: (0, 0)),
        _lspec((1, width), layer, lambda i: (0, 0)),
        _lspec((ngrp, SGU_CHUNK, SGU_CHUNK), layer, lambda i: (0, 0, 0)),
        _lspec((SGU_CHUNK, width), layer, lambda i: (0, 0)),
        pl.BlockSpec((None, width, d), lambda i: (layer, 0, 0), pipeline_mode=pl.Buffered(1)),
    ]
    args = [su, su, lng, lnb, mix, bias, w_out_a]
    aliases = _chain(args, in_specs, prev, 1)
    return pl.pallas_call(
        functools.partial(_sgu_body, emit_v=emit_v, chained=prev is not None),
        grid=(m // rows,),
        in_specs=in_specs,
        out_specs=out_specs,
        out_shape=out_shape,
        scratch_shapes=[pltpu.VMEM((rows, width), BF16)],
        input_output_aliases=aliases,
        compiler_params=pltpu.CompilerParams(
            dimension_semantics=("arbitrary",), vmem_limit_bytes=_vmem_limit(est)),
        name="sgu",
    )(*args)


def _prefix_sum_rows(x, row):
    n = x.shape[0]
    s = 1
    while s < n:
        x = x + jnp.where(row >= s, pltpu.roll(x, s, 0), 0.0)
        s *= 2
    return x


def _gate_groupnorm(y, z, ng):
    t = y * _silu(z)
    return t * lax.rsqrt(jnp.mean(t * t, axis=-1, keepdims=True) + EPS) * ng


def _split_hi_lo(v):
    hi = v.astype(BF16)
    lo = (v - hi.astype(F32)).astype(BF16)
    return jnp.concatenate([hi, lo], axis=1)


def _ssd_prompt_body(xbc_ref, z_ref, dt_ref, cw_ref, cb_ref, alog_ref, dexp_ref, ng_ref, e2_ref, *rest,
                     d_inner, n_chunks, chained):
    if chained:
        rest = rest[2:]
    yb_ref, h_ref, cst_ref, win_scr, act_scr, y_scr, ht_scr, cst_scr = rest
    c = pl.program_id(1)
    q = xbc_ref.shape[0]
    conv_dim = xbc_ref.shape[1]
    p = SSM_HEAD_DIM
    n_state = SSM_STATE
    gw = d_inner // SSM_GROUPS
    pairs_per_group = gw // (2 * p)
    halo = SUBLANES

    @pl.when(c == 0)
    def _():
        win_scr[0:halo, :] = jnp.zeros((halo, conv_dim), F32)
        ht_scr[...] = jnp.zeros(ht_scr.shape, F32)

    win_scr[halo:halo + q, :] = xbc_ref[...].astype(F32)
    lane_chunk = 512
    for j in range(conv_dim // lane_chunk):
        ls = slice(j * lane_chunk, (j + 1) * lane_chunk)
        acc = cb_ref[:, ls] + cw_ref[0:1, ls] * win_scr[halo - 3:halo - 3 + q, ls]
        for k in range(1, CONV_WIDTH):
            acc = acc + cw_ref[k:k + 1, ls] * win_scr[halo - 3 + k:halo - 3 + k + q, ls]
        act_scr[:, ls] = _silu(acc)

    @pl.when(c == n_chunks - 1)
    def _():
        cst_ref[...] = win_scr[halo + q - 3:halo + q, :]

    win_scr[0:halo, :] = win_scr[q:q + halo, :]

    row = lax.broadcasted_iota(jnp.int32, (q, LANES), 0)
    lane = lax.broadcasted_iota(jnp.int32, (q, LANES), 1)
    dt = dt_ref[...]
    da = dt * (-jnp.exp(alog_ref[...]))
    cs = _prefix_sum_rows(da, row)
    cst_scr[...] = cs.T
    dt_hl = _split_hi_lo(dt)
    ecs_hl = _split_hi_lo(jnp.exp(cs))
    dec_hl = _split_hi_lo(jnp.exp(cs[q - 1:q, :] - cs))
    causal = row >= lane
    lower_half = lane < p

    for grp in range(SSM_GROUPS):
        gs = slice(grp * gw, (grp + 1) * gw)
        e2 = e2_ref[:, gs]
        dt_g = _dot(dt_hl, e2)
        ecs_g = _dot(ecs_hl, e2)
        dec_g = _dot(dec_hl, e2)
        xs_g = act_scr[:, gs]
        xdt_g = xs_g * dt_g
        b_g = act_scr[:, d_inner + grp * n_state:d_inner + (grp + 1) * n_state].astype(BF16)
        c_g = act_scr[:, d_inner + (SSM_GROUPS + grp) * n_state:
                      d_inner + (SSM_GROUPS + grp + 1) * n_state].astype(BF16)
        cbm = jnp.where(causal, _dot_nt(c_g, b_g), 0.0)
        ht_g = ht_scr[:, gs]
        y_g = _dot(c_g, ht_g.astype(BF16)) * ecs_g + dexp_ref[:, gs] * xs_g
        for j in range(pairs_per_group):
            h0 = grp * 2 * pairs_per_group + 2 * j
            ws = []
            for h in (h0, h0 + 1):
                seg = cs[:, h:h + 1] - cst_scr[h:h + 1, :]
                ws.append((cbm * jnp.exp(jnp.where(causal, seg, 0.0))).astype(BF16))
            xpair = xdt_g[:, j * LANES:(j + 1) * LANES]
            rhs = jnp.concatenate([jnp.where(lower_half, xpair, 0.0).astype(BF16),
                                   jnp.where(lower_half, 0.0, xpair).astype(BF16)], axis=0)
            ls = slice(grp * gw + j * LANES, grp * gw + (j + 1) * LANES)
            y_scr[:, ls] = y_g[:, j * LANES:(j + 1) * LANES] + _dot(jnp.concatenate(ws, axis=1), rhs)
        states_t = _dot_tn(b_g, (xdt_g * dec_g).astype(BF16))
        ht_scr[:, gs] = ht_g * ecs_g[q - 1:q, :] + states_t

    for grp in range(SSM_GROUPS):
        gs = slice(grp * gw, (grp + 1) * gw)
        yb_ref[:, gs] = _gate_groupnorm(y_scr[:, gs], z_ref[:, gs].astype(F32), ng_ref[:, gs]).astype(yb_ref.dtype)

    @pl.when(c == n_chunks - 1)
    def _():
        for grp in range(SSM_GROUPS):
            gs = slice(grp * gw, (grp + 1) * gw)
            h_ref[gs, :] = ht_scr[:, gs].T


def _ssd_prompt(xbc, z, dt, conv_w, conv_b, a_log, d_exp, norm_g, e2, batch, seq, depth, layer, prev):
    q = SSD_CHUNK
    n_chunks = seq // q
    conv_dim = xbc.shape[1]
    d_inner = z.shape[1]
    dtw = dt.shape[1]
    est = (2 * (q * conv_dim * 2 + q * d_inner * 2 + q * dtw * 4 + q * d_inner * 2 + d_inner * SSM_STATE * 4
                + 8 * conv_dim * 4 + 5 * conv_dim * 4 + 2 * d_inner * 4 + 2 * LANES * d_inner * 2)
           + (q + 16) * conv_dim * 4 + q * conv_dim * 4 + 2 * q * d_inner * 4 + (8 << 20))
    row_map = lambda b, c: (b * n_chunks + c, 0)
    const = lambda b, c: (0, 0)
    in_specs = [
        pl.BlockSpec((q, conv_dim), row_map),
        pl.BlockSpec((q, d_inner), row_map),
        pl.BlockSpec((q, dtw), row_map),
        _lspec((CONV_WIDTH, conv_dim), layer, const),
        _lspec((1, conv_dim), layer, const),
        _lspec((1, dtw), layer, const),
        _lspec((1, d_inner), layer, const),
        _lspec((1, d_inner), layer, const),
        pl.BlockSpec((2 * LANES, d_inner), const),
    ]
    args = [xbc, z, dt, conv_w, conv_b, a_log, d_exp, norm_g, e2]
    aliases = _chain(args, in_specs, prev, 1)
    return pl.pallas_call(
        functools.partial(_ssd_prompt_body, d_inner=d_inner, n_chunks=n_chunks, chained=prev is not None),
        grid=(batch, n_chunks),
        in_specs=in_specs,
        out_specs=[
            pl.BlockSpec((q, d_inner), row_map),
            pl.BlockSpec((None, None, d_inner, SSM_STATE), lambda b, c: (layer, b, 0, 0)),
            pl.BlockSpec((None, None, CONV_WIDTH - 1, conv_dim), lambda b, c: (layer, b, 0, 0)),
        ],
        out_shape=[
            jax.ShapeDtypeStruct((batch * seq, d_inner), BF16),
            jax.ShapeDtypeStruct((depth, batch, d_inner, SSM_STATE), F32),
            jax.ShapeDtypeStruct((depth, batch, CONV_WIDTH - 1, conv_dim), F32),
        ],
        scratch_shapes=[
            pltpu.VMEM((q + 2 * SUBLANES, conv_dim), F32),
            pltpu.VMEM((q, conv_dim), F32),
            pltpu.VMEM((q, d_inner), F32),
            pltpu.VMEM((SSM_STATE, d_inner), F32),
            pltpu.VMEM((LANES, q), F32),
        ],
        input_output_aliases=aliases,
        compiler_params=pltpu.CompilerParams(
            dimension_semantics=("arbitrary", "arbitrary"), vmem_limit_bytes=_vmem_limit(est)),
        name="ssd_prompt",
    )(*args)


def _ssd_sample_body(xbc_ref, z_ref, dte_ref, cbuf_ref, h0_ref, cw_ref, cb_ref, aexp_ref, dexp_ref, ng_ref,
                     *rest, d_inner, steps, chained):
    if chained:
        rest = rest[2:]
    yb_ref, h_ref, cst_ref, win_scr = rest
    q = xbc_ref.shape[0]
    nseq = q // steps
    conv_dim = xbc_ref.shape[1]
    n_state = SSM_STATE
    gw = d_inner // SSM_GROUPS
    halo = SUBLANES
    span = 2 * SUBLANES

    win_scr[...] = jnp.zeros(win_scr.shape, F32)
    for j in range(nseq):
        win_scr[j * span + halo - 3:j * span + halo, :] = cbuf_ref[j]
        win_scr[j * span + halo:j * span + halo + steps, :] = xbc_ref[j * steps:(j + 1) * steps, :]
        cst_ref[j] = win_scr[j * span + halo + steps - 3:j * span + halo + steps, :]
    rowc = lax.broadcasted_iota(jnp.int32, (q, conv_dim), 0)
    acc = None
    for j in range(nseq):
        base = j * span + halo - 3 - j * steps
        acc_j = cb_ref[...] + cw_ref[0:1, :] * win_scr[base:base + q, :]
        for k in range(1, CONV_WIDTH):
            acc_j = acc_j + cw_ref[k:k + 1, :] * win_scr[base + k:base + k + q, :]
        acc = acc_j if acc is None else jnp.where(rowc >= j * steps, acc_j, acc)
    act = _silu(acc)

    xs = act[:, :d_inner]
    row = lax.broadcasted_iota(jnp.int32, (q, d_inner), 0)
    tok = row % steps
    dt = dte_ref[...]
    da = dt * aexp_ref[...]
    cs = da
    s = 1
    while s < steps:
        cs = cs + jnp.where(tok >= s, pltpu.roll(cs, s, 0), 0.0)
        s *= 2
    cs_last = cs[steps - 1:steps, :]
    for j in range(1, nseq):
        cs_last = jnp.where(row >= j * steps, cs[(j + 1) * steps - 1:(j + 1) * steps, :], cs_last)
    xdt = xs * dt

    b_all = act[:, d_inner:d_inner + SSM_GROUPS * n_state]
    c_all = act[:, d_inner + SSM_GROUPS * n_state:]
    y = dexp_ref[...] * xs
    for o in range(steps):
        b_sh = b_all if o == 0 else pltpu.roll(b_all, o, 0)
        cs_sh = cs if o == 0 else pltpu.roll(cs, o, 0)
        xdt_sh = xdt if o == 0 else pltpu.roll(xdt, o, 0)
        prod = c_all * b_sh
        cb_o = jnp.concatenate(
            [jnp.broadcast_to(jnp.sum(prod[:, grp * n_state:(grp + 1) * n_state], axis=1, keepdims=True), (q, gw))
             for grp in range(SSM_GROUPS)], axis=1)
        keep = tok >= o
        lmat = jnp.where(keep, jnp.exp(jnp.where(keep, cs - cs_sh, 0.0)), 0.0)
        y = y + cb_o * lmat * xdt_sh

    ecs = jnp.exp(cs)
    xs_dec = xdt * jnp.exp(cs_last - cs)
    cd = jnp.exp(cs_last)
    rowg = lax.broadcasted_iota(jnp.int32, (q, gw), 0)
    lane = lax.broadcasted_iota(jnp.int32, (q, LANES), 1)
    p = SSM_HEAD_DIM
    for grp in range(SSM_GROUPS):
        gs = slice(grp * gw, (grp + 1) * gw)
        b_g = b_all[:, grp * n_state:(grp + 1) * n_state].astype(BF16)
        c_g = c_all[:, grp * n_state:(grp + 1) * n_state].astype(BF16)
        head_cd = []
        for t in range(gw // LANES):
            v = cd[:, grp * gw + t * LANES:grp * gw + (t + 1) * LANES]
            vr = pltpu.roll(v, p, 1)
            head_cd += [jnp.where(lane < p, v, vr), jnp.where(lane < p, vr, v)]
        y_off = None
        for j in range(nseq):
            h_g = h0_ref[j, gs, :]
            mine = (rowg >= j * steps) & (rowg < (j + 1) * steps)
            y_off_j = _dot_nt(c_g, h_g.astype(BF16))
            y_off = y_off_j if y_off is None else jnp.where(mine, y_off_j, y_off)
            states = _dot_tn(jnp.where(mine, xs_dec[:, gs], 0.0).astype(BF16), b_g)
            scale = jnp.concatenate(
                [jnp.broadcast_to(hc[j * steps:j * steps + 1, :], (p, n_state)) for hc in head_cd], axis=0)
            h_ref[j, gs, :] = h_g * scale + states
        y_g = y[:, gs] + y_off * ecs[:, gs]
        yb_ref[:, gs] = _gate_groupnorm(y_g, z_ref[:, gs].astype(F32), ng_ref[:, gs]).astype(yb_ref.dtype)


def _ssd_sample(xbc, z, dte, conv_state, h0_all, layer, conv_w, conv_b, a_exp, d_exp, norm_g, steps, prev):
    m, conv_dim = xbc.shape
    d_inner = z.shape[1]
    depth, nseq_total = h0_all.shape[:2]
    q = SUBLANES
    nseq = q // steps
    est = (2 * (2 * nseq * d_inner * SSM_STATE * 4 + q * (conv_dim + 3 * d_inner) * 4
                + 2 * nseq * SUBLANES * conv_dim * 4 + 5 * conv_dim * 4 + 3 * d_inner * 4)
           + 2 * nseq * SUBLANES * conv_dim * 4 + d_inner * LANES * 4 + (12 << 20))
    row_map = lambda s: (s, 0)
    lay_map = lambda s: (layer, s, 0, 0)
    const = lambda s: (0, 0)
    in_specs = [
        pl.BlockSpec((q, conv_dim), row_map),
        pl.BlockSpec((q, d_inner), row_map),
        pl.BlockSpec((q, d_inner), row_map),
        pl.BlockSpec((None, nseq, CONV_WIDTH - 1, conv_dim), lay_map),
        pl.BlockSpec((None, nseq, d_inner, SSM_STATE), lay_map),
        _lspec((CONV_WIDTH, conv_dim), layer, const),
        _lspec((1, conv_dim), layer, const),
        _lspec((1, d_inner), layer, const),
        _lspec((1, d_inner), layer, const),
        _lspec((1, d_inner), layer, const),
    ]
    args = [xbc, z, dte, conv_state, h0_all, conv_w, conv_b, a_exp, d_exp, norm_g]
    aliases = _chain(args, in_specs, prev, 1)
    return pl.pallas_call(
        functools.partial(_ssd_sample_body, d_inner=d_inner, steps=steps, chained=prev is not None),
        grid=(m // q,),
        in_specs=in_specs,
        out_specs=[
            pl.BlockSpec((q, d_inner), row_map),
            pl.BlockSpec((None, nseq, d_inner, SSM_STATE), lay_map),
            pl.BlockSpec((None, nseq, CONV_WIDTH - 1, conv_dim), lay_map),
        ],
        out_shape=[
            jax.ShapeDtypeStruct((m, d_inner), F32),
            jax.ShapeDtypeStruct((depth, nseq_total, d_inner, SSM_STATE), F32),
            jax.ShapeDtypeStruct((depth, nseq_total, CONV_WIDTH - 1, conv_dim), F32),
        ],
        scratch_shapes=[pltpu.VMEM((nseq * 2 * SUBLANES, conv_dim), F32)],
        input_output_aliases=aliases,
        compiler_params=pltpu.CompilerParams(
            dimension_semantics=("arbitrary",), vmem_limit_bytes=_vmem_limit(est)),
        name="ssd_sample",
    )(*args)


def _merge_body(a_ref, yb_ref, ga_ref, gb_ref, x_ref, wb_ref, wo_ref, o_ref):
    b = _dot(yb_ref[...].astype(BF16), wb_ref[...])
    merged = ga_ref[...].astype(F32) * a_ref[...].astype(F32) + gb_ref[...].astype(F32) * b
    o_ref[...] = x_ref[...] + _dot(merged.astype(BF16), wo_ref[...])


def _merge(a, yb, gates, x, w_out_b, w_o, layer, rows):
    m, d = x.shape
    d_inner = yb.shape[1]
    est = (2 * rows * (d * 2 + d_inner * yb.dtype.itemsize + 2 * d * gates.dtype.itemsize + 2 * d * 4)
           + d_inner * d * 2 + d * d * 2 + 3 * rows * d * 4)
    return pl.pallas_call(
        _merge_body,
        grid=(m // rows,),
        in_specs=[
            pl.BlockSpec((rows, d), lambda i: (i, 0)),
            pl.BlockSpec((rows, d_inner), lambda i: (i, 0)),
            pl.BlockSpec((rows, d), lambda i: (i, 0)),
            pl.BlockSpec((rows, d), lambda i: (i, 1)),
            pl.BlockSpec((rows, d), lambda i: (i, 0)),
            pl.BlockSpec((None, d_inner, d), lambda i: (layer, 0, 0), pipeline_mode=pl.Buffered(1)),
            pl.BlockSpec((None, d, d), lambda i: (layer, 0, 0), pipeline_mode=pl.Buffered(1)),
        ],
        out_specs=pl.BlockSpec((rows, d), lambda i: (i, 0)),
        out_shape=jax.ShapeDtypeStruct((m, d), F32),
        compiler_params=pltpu.CompilerParams(
            dimension_semantics=("arbitrary",), vmem_limit_bytes=_vmem_limit(est)),
        name="merge",
    )(a, yb, gates, gates, x, w_out_b, w_o)


def _ffn_body(x_ref, g_ref, wg_ref, wu_ref, wd_ref, fg_ref, o_ref, h_scr, *, n_f, final_norm, sub):
    f = pl.program_id(1)
    blocks = _row_blocks(x_ref.shape[0], sub)

    @pl.when(f == 0)
    def _():
        for rs in blocks:
            x = x_ref[rs, :]
            h_scr[rs, :] = _rmsnorm(x, g_ref[...]).astype(BF16)
            o_ref[rs, :] = x

    for rs in blocks:
        hb = h_scr[rs, :]
        act = _silu(_dot(hb, wg_ref[...])) * _dot(hb, wu_ref[...])
        o_ref[rs, :] += _dot(act.astype(BF16), wd_ref[...])

    if final_norm:
        @pl.when(f == n_f - 1)
        def _():
            for rs in blocks:
                o_ref[rs, :] = _rmsnorm(o_ref[rs, :], fg_ref[...])


def _ffn(x, g, w_gate, w_up, w_down, final_g, layer, tm, tf, final_norm):
    m, d = x.shape
    hidden = w_gate.shape[2]
    n_f = hidden // tf
    est = 2 * (2 * tm * d * 4 + 3 * d * tf * 2) + tm * d * 2
    return pl.pallas_call(
        functools.partial(_ffn_body, n_f=n_f, final_norm=final_norm, sub=256),
        grid=(m // tm, n_f),
        in_specs=[
            pl.BlockSpec((tm, d), lambda i, j: (i, 0)),
            _lspec((1, d), layer, lambda i, j: (0, 0)),
            _lspec((d, tf), layer, lambda i, j: (0, j)),
            _lspec((d, tf), layer, lambda i, j: (0, j)),
            _lspec((tf, d), layer, lambda i, j: (j, 0)),
            pl.BlockSpec((1, d), lambda i, j: (0, 0)),
        ],
        out_specs=pl.BlockSpec((tm, d), lambda i, j: (i, 0)),
        out_shape=jax.ShapeDtypeStruct((m, d), F32),
        scratch_shapes=[pltpu.VMEM((tm, d), BF16)],
        compiler_params=pltpu.CompilerParams(
            dimension_semantics=("arbitrary", "arbitrary"), vmem_limit_bytes=_vmem_limit(est)),
        name="ffn",
    )(x, g, w_gate, w_up, w_down, final_g)


def _row_tile(m, target):
    t = min(m, target)
    while m % t:
        t //= 2
    return t


def kernel(x_prompt, x_sample, state_ssm, state_conv, norm1_g, w_in, conv_w, conv_b, dt_bias, a_log, d_skip, ssm_norm_g, sgu_ln_g, sgu_ln_b, sgu_w, sgu_b, w_out_a, w_out_b, w_o, norm2_g, w_ffn_gate, w_ffn_up, w_ffn_down, final_norm_g):
    batch, seq, d = x_prompt.shape
    nseq, steps, _ = x_sample.shape
    depth = w_in.shape[0]
    heads = dt_bias.shape[1]
    d_inner = heads * SSM_HEAD_DIM
    conv_dim = conv_w.shape[2]
    sgu_width = sgu_ln_g.shape[1]
    ngrp = sgu_w.shape[1]
    seg_widths = (2 * sgu_width, d_inner, conv_dim, 2 * d)
    dt_col = 2 * sgu_width + d_inner + conv_dim

    xp = x_prompt.reshape(batch * seq, d)
    xs = x_sample.reshape(nseq * steps, d)
    fin_g = final_norm_g[None, :]
    state_ssm_flat = state_ssm.reshape(depth, nseq, d_inner, SSM_STATE)

    def rows3(p):
        return p.reshape(depth, 1, p.shape[-1])

    w_in_b = w_in.astype(BF16)
    w_gates_b = w_in[:, :, dt_col + heads:].astype(BF16)
    dt_b = rows3(jnp.pad(dt_bias, ((0, 0), (0, LANES - heads))))
    w_dt_exp = jnp.repeat(w_in[:, :, dt_col:dt_col + heads], SSM_HEAD_DIM, axis=2).astype(BF16)
    dt_b_exp = rows3(jnp.repeat(dt_bias, SSM_HEAD_DIM, axis=1))
    a_log_pad = rows3(jnp.pad(a_log, ((0, 0), (0, LANES - heads))))
    a_exp = rows3(jnp.repeat(-jnp.exp(a_log), SSM_HEAD_DIM, axis=1))
    d_exp = rows3(jnp.repeat(d_skip, SSM_HEAD_DIM, axis=1))
    n1, n2, ng = rows3(norm1_g), rows3(norm2_g), rows3(ssm_norm_g)
    cb, lng, lnb = rows3(conv_b), rows3(sgu_ln_g), rows3(sgu_ln_b)
    tril = jnp.tril(jnp.ones((SGU_CHUNK, SGU_CHUNK), bool))
    mix_p = jnp.where(tril, sgu_w, 0).astype(BF16)
    bias_p = jnp.repeat(jnp.swapaxes(sgu_b, 1, 2), SGU_GROUP_DIM, axis=2)
    seqs_per_chunk = SGU_CHUNK // steps
    eye = jnp.eye(seqs_per_chunk, dtype=F32)
    w_small = jnp.where(jnp.tril(jnp.ones((steps, steps), bool)), sgu_w[:, :, :steps, :steps], 0)
    mix_s = jnp.einsum("ab,lgts->lgatbs", eye, w_small).reshape(depth, ngrp, SGU_CHUNK, SGU_CHUNK).astype(BF16)
    bias_s = jnp.tile(bias_p[:, :steps], (1, seqs_per_chunk, 1))
    wa, wb, wo = w_out_a.astype(BF16), w_out_b.astype(BF16), w_o.astype(BF16)
    wg, wu, wd = w_ffn_gate.astype(BF16), w_ffn_up.astype(BF16), w_ffn_down.astype(BF16)
    expand = (jnp.arange(d_inner)[None, :] // SSM_HEAD_DIM == jnp.arange(LANES)[:, None]).astype(BF16)
    e2 = jnp.concatenate([expand, expand], axis=0)

    tm_p = _row_tile(xp.shape[0], 1024)
    tm_s = _row_tile(xs.shape[0], 512)
    prev_p = prev_s = prev_v = None
    for l in range(depth):
        last = l == depth - 1

        su, z, xbc, gt, dt = _in_proj(xp, n1, w_in_b, w_gates_b, dt_b, l, seg_widths, BF16, tm_p, 1024)
        (a,) = _sgu(su, lng, lnb, mix_p, bias_p, wa, l, _row_tile(xp.shape[0], 512), False)
        yb, *prev_p = _ssd_prompt(xbc, z, dt, conv_w, cb, a_log_pad, d_exp, ng, e2, batch, seq, depth, l, prev_p)
        xp = _merge(a, yb, gt, xp, wb, wo, l, _row_tile(xp.shape[0], 256))
        xp = _ffn(xp, n2, wg, wu, wd, fin_g, l, tm_p, 512, last)

        su, z, xbc, gt, _ = _in_proj(xs, n1, w_in_b, w_gates_b, dt_b, l, seg_widths, F32, tm_s, 1024)
        dte = _dt_expand(xs, n1, w_dt_exp, dt_b_exp, l, 1024)
        a, *prev_v = _sgu(su, lng, lnb, mix_s, bias_s, wa, l, tm_s, True, prev_v)
        yb, *prev_s = _ssd_sample(xbc, z, dte, state_conv, state_ssm_flat, l, conv_w, cb, a_exp, d_exp, ng,
                                  steps, prev_s)
        xs = _merge(a, yb, gt, xs, wb, wo, l, _row_tile(xs.shape[0], 256))
        xs = _ffn(xs, n2, wg, wu, wd, fin_g, l, tm_s, 512, last)

    ssm_p, conv_p = prev_p
    ssm_s, conv_s = prev_s
    (v_s,) = prev_v
    return (xp.reshape(batch, seq, d), xs.reshape(nseq, steps, d),
            ssm_p.reshape(depth, batch, heads, SSM_HEAD_DIM, SSM_STATE), conv_p,
            ssm_s.reshape(depth, nseq, heads, SSM_HEAD_DIM, SSM_STATE), conv_s,
            v_s.reshape(depth, nseq, steps, sgu_width))
```

```python
import functools

import jax
import jax.numpy as jnp
from jax import lax
from jax.experimental import pallas as pl
from jax.experimental.pallas import tpu as pltpu

F32 = jnp.float32
BF16 = jnp.bfloat16
EPS = 1e-6
LOG2_E = 1.4426950408889634

LANES = 128
SUBLANES = 8
VMEM_CAP_BYTES = 64 * 1024 * 1024

SGU_CHUNK = 128
SGU_GROUP_DIM = 128
SSM_HEAD_DIM = 64
SSM_STATE = 128
SSM_GROUPS = 8
CONV_WIDTH = 4
SSD_CHUNK = 128


def _vmem_limit(estimate_bytes):
    return int(min(estimate_bytes + (8 << 20), VMEM_CAP_BYTES - (6 << 20)))


def _silu(x):
    return x * jax.nn.sigmoid(x)


def _softplus(x):
    return jnp.maximum(x, 0.0) + jnp.log1p(jnp.exp(-jnp.abs(x)))


def _rmsnorm(x, g):
    return x * lax.rsqrt(jnp.mean(x * x, axis=-1, keepdims=True) + EPS) * g


def _dot(a, b):
    return jnp.dot(a, b, preferred_element_type=F32)


def _dot_nt(a, b):
    return lax.dot_general(a, b, (((1,), (1,)), ((), ())), preferred_element_type=F32)


def _dot_tn(a, b):
    return lax.dot_general(a, b, (((0,), (0,)), ((), ())), preferred_element_type=F32)


def _chain(args, in_specs, prev, first_out):
    aliases = {}
    if prev is not None:
        for k, arr in enumerate(prev):
            aliases[len(args)] = first_out + k
            args.append(arr)
            in_specs.append(pl.BlockSpec(memory_space=pl.ANY))
    return aliases


def _row_blocks(rows, sub):
    sub = min(rows, sub)
    return [slice(r * sub, (r + 1) * sub) for r in range(rows // sub)]


def _lspec(block, layer, tail_index):
    return pl.BlockSpec((None,) + tuple(block), lambda *ids: (layer,) + tuple(tail_index(*ids)))


def _norm_body(x_ref, g_ref, o_ref):
    o_ref[...] = _rmsnorm(x_ref[...], g_ref[...]).astype(o_ref.dtype)


def _norm(x, g, layer, rows):
    m, d = x.shape
    est = 2 * rows * d * 6 + rows * d * 8
    return pl.pallas_call(
        _norm_body,
        grid=(m // rows,),
        in_specs=[pl.BlockSpec((rows, d), lambda i: (i, 0)), _lspec((1, d), layer, lambda i: (0, 0))],
        out_specs=pl.BlockSpec((rows, d), lambda i: (i, 0)),
        out_shape=jax.ShapeDtypeStruct((m, d), BF16),
        compiler_params=pltpu.CompilerParams(
            dimension_semantics=("arbitrary",), vmem_limit_bytes=_vmem_limit(est)),
        name="norm",
    )(x, g)


def _proj_body(h_ref, w_ref, *rest, act, sub, cast, biased):
    rest = list(rest)
    b_ref = rest.pop(0) if biased else None
    o_ref = rest.pop(0)
    if cast:
        (w_bf,) = rest

        @pl.when(pl.program_id(1) == 0)
        def _():
            w_bf[...] = w_ref[...].astype(BF16)
    else:
        w_bf = w_ref
    for rs in _row_blocks(h_ref.shape[0], sub):
        t = _dot(h_ref[rs, :], w_bf[...])
        if biased:
            t = t + b_ref[...]
        o_ref[rs, :] = act(t).astype(o_ref.dtype)


def _proj(h, w, layer, col0, width, act, out_dtype, tm, tn, bias=None):
    m, d = h.shape
    cast = w.dtype != BF16
    j0 = col0 // tn
    osz = jnp.dtype(out_dtype).itemsize
    est = 2 * (tm * d * 2 + d * tn * w.dtype.itemsize + tm * tn * osz + tn * 4) + d * tn * 2 + 256 * tn * 8
    in_specs = [
        pl.BlockSpec((tm, d), lambda j, i: (i, 0)),
        _lspec((d, tn), layer, lambda j, i: (0, j0 + j)),
    ]
    args = [h, w]
    if bias is not None:
        in_specs.append(_lspec((1, tn), layer, lambda j, i: (0, j)))
        args.append(bias)
    return pl.pallas_call(
        functools.partial(_proj_body, act=act, sub=256, cast=cast, biased=bias is not None),
        grid=(width // tn, m // tm),
        in_specs=in_specs,
        out_specs=pl.BlockSpec((tm, tn), lambda j, i: (i, j)),
        out_shape=jax.ShapeDtypeStruct((m, width), out_dtype),
        scratch_shapes=[pltpu.VMEM((d, tn), BF16)] if cast else [],
        compiler_params=pltpu.CompilerParams(
            dimension_semantics=("arbitrary", "arbitrary"), vmem_limit_bytes=_vmem_limit(est)),
        name="proj",
    )(*args)


def _identity(t):
    return t


def _sgu_body(u_ref, v_ref, lng_ref, lnb_ref, mix_ref, bias_ref, wa_ref, *rest, emit_v, chained):
    if emit_v:
        if chained:
            rest = rest[1:]
        a_ref, vout_ref, ya_scr = rest
    else:
        a_ref, ya_scr = rest
    vg = v_ref[...].astype(F32)
    mu = jnp.mean(vg, axis=-1, keepdims=True)
    var = jnp.mean(jnp.square(vg - mu), axis=-1, keepdims=True)
    v = (vg - mu) * lax.rsqrt(var + EPS) * lng_ref[...] + lnb_ref[...]
    if emit_v:
        vout_ref[...] = v
    vb = v.astype(BF16)
    rows, width = vb.shape
    for c in range(rows // SGU_CHUNK):
        rs = slice(c * SGU_CHUNK, (c + 1) * SGU_CHUNK)
        for grp in range(width // SGU_GROUP_DIM):
            cs = slice(grp * SGU_GROUP_DIM, (grp + 1) * SGU_GROUP_DIM)
            mixed = _dot(mix_ref[grp], vb[rs, cs]) + bias_ref[:, cs]
            ya_scr[rs, cs] = (u_ref[rs, cs].astype(F32) * mixed).astype(BF16)
    a_ref[...] = _dot(ya_scr[...], wa_ref[...]).astype(a_ref.dtype)


def _sgu(su, lng, lnb, mix, bias, w_out_a, layer, rows, emit_v, prev=None):
    m = su.shape[0]
    width = su.shape[1] // 2
    depth, _, d = w_out_a.shape
    ngrp = mix.shape[1]
    isz = su.dtype.itemsize
    est = 2 * (2 * rows * width * isz + ngrp * SGU_CHUNK * SGU_CHUNK * 2 + SGU_CHUNK * width * 4
               + rows * d * 2 + rows * width * 4) + width * d * 2 + 4 * rows * width * 4
    out_shape = [jax.ShapeDtypeStruct((m, d), BF16)]
    out_specs = [pl.BlockSpec((rows, d), lambda i: (i, 0))]
    if emit_v:
        out_shape.append(jax.ShapeDtypeStruct((depth, m, width), F32))
        out_specs.append(pl.BlockSpec((None, rows, width), lambda i: (layer, i, 0)))
    in_specs = [
        pl.BlockSpec((rows, width), lambda i: (i, 0)),
        pl.BlockSpec((rows, width), lambda i: (i, 1)),
        _lspec((1, width), layer, lambda i: (0, 0)),
        _lspec((1, width), layer, lambda i: (0, 0)),
        _lspec((ngrp, SGU_CHUNK, SGU_CHUNK), layer, lambda i: (0, 0, 0)),
        _lspec((SGU_CHUNK, width), layer, lambda i: (0, 0)),
        pl.BlockSpec((None, width, d), lambda i: (layer, 0, 0), pipeline_mode=pl.Buffered(1)),
    ]
    args = [su, su, lng, lnb, mix, bias, w_out_a]
    aliases = _chain(args, in_specs, prev, 1)
    return pl.pallas_call(
        functools.partial(_sgu_body, emit_v=emit_v, chained=prev is not None),
        grid=(m // rows,),
        in_specs=in_specs,
        out_specs=out_specs,
        out_shape=out_shape,
        scratch_shapes=[pltpu.VMEM((rows, width), BF16)],
        input_output_aliases=aliases,
        compiler_params=pltpu.CompilerParams(
            dimension_semantics=("arbitrary",), vmem_limit_bytes=_vmem_limit(est)),
        name="sgu",
    )(*args)


def _prefix_sum_rows(x, row):
    n = x.shape[0]
    s = 1
    while s < n:
        x = x + jnp.where(row >= s, pltpu.roll(x, s, 0), 0.0)
        s *= 2
    return x


def _gate_groupnorm(y, z, ng):
    t = y * _silu(z)
    return t * lax.rsqrt(jnp.mean(t * t, axis=-1, keepdims=True) + EPS) * ng


def _split_hi_lo(v):
    hi = v.astype(BF16)
    lo = (v - hi.astype(F32)).astype(BF16)
    return jnp.concatenate([hi, lo], axis=1)


def _ssd_prompt_body(xbc_ref, z_ref, dt_ref, cw_ref, cb_ref, alog_ref, dexp_ref, ng_ref, e2_ref, *rest,
                     d_inner, n_chunks, chained):
    if chained:
        rest = rest[2:]
    yb_ref, h_ref, cst_ref, win_scr, act_scr, y_scr, ht_scr, cst_scr = rest
    c = pl.program_id(1)
    q = xbc_ref.shape[0]
    conv_dim = xbc_ref.shape[1]
    p = SSM_HEAD_DIM
    n_state = SSM_STATE
    gw = d_inner // SSM_GROUPS
    pairs_per_group = gw // (2 * p)
    halo = SUBLANES

    @pl.when(c == 0)
    def _():
        win_scr[0:halo, :] = jnp.zeros((halo, conv_dim), F32)
        ht_scr[...] = jnp.zeros(ht_scr.shape, F32)

    win_scr[halo:halo + q, :] = xbc_ref[...].astype(F32)
    lane_chunk = 512
    for j in range(conv_dim // lane_chunk):
        ls = slice(j * lane_chunk, (j + 1) * lane_chunk)
        acc = cb_ref[:, ls] + cw_ref[0:1, ls] * win_scr[halo - 3:halo - 3 + q, ls]
        for k in range(1, CONV_WIDTH):
            acc = acc + cw_ref[k:k + 1, ls] * win_scr[halo - 3 + k:halo - 3 + k + q, ls]
        act_scr[:, ls] = _silu(acc)

    @pl.when(c == n_chunks - 1)
    def _():
        cst_ref[...] = win_scr[halo + q - 3:halo + q, :]

    win_scr[0:halo, :] = win_scr[q:q + halo, :]

    row = lax.broadcasted_iota(jnp.int32, (q, LANES), 0)
    lane = lax.broadcasted_iota(jnp.int32, (q, LANES), 1)
    dt = dt_ref[...]
    da = dt * (-jnp.exp(alog_ref[...]))
    cs = _prefix_sum_rows(da, row)
    cs2 = cs * LOG2_E
    cst_scr[...] = cs2.T
    dt_hl = _split_hi_lo(dt)
    ecs_hl = _split_hi_lo(jnp.exp(cs))
    dec_hl = _split_hi_lo(jnp.exp(cs[q - 1:q, :] - cs))
    causal = row >= lane
    lower_half = lane < p

    for grp in range(SSM_GROUPS):
        gs = slice(grp * gw, (grp + 1) * gw)
        e2 = e2_ref[:, gs]
        dt_g = _dot(dt_hl, e2)
        ecs_g = _dot(ecs_hl, e2)
        dec_g = _dot(dec_hl, e2)
        xs_g = act_scr[:, gs]
        xdt_g = xs_g * dt_g
        b_g = act_scr[:, d_inner + grp * n_state:d_inner + (grp + 1) * n_state].astype(BF16)
        c_g = act_scr[:, d_inner + (SSM_GROUPS + grp) * n_state:
                      d_inner + (SSM_GROUPS + grp + 1) * n_state].astype(BF16)
        cbm = jnp.where(causal, _dot_nt(c_g, b_g), 0.0)
        ht_g = ht_scr[:, gs]
        y_g = _dot(c_g, ht_g.astype(BF16)) * ecs_g + dexp_ref[:, gs] * xs_g
        for j in range(pairs_per_group):
            h0 = grp * 2 * pairs_per_group + 2 * j
            ws = []
            for h in (h0, h0 + 1):
                seg2 = cs2[:, h:h + 1] - cst_scr[h:h + 1, :]
                ws.append((cbm * jnp.exp2(jnp.where(causal, seg2, 0.0))).astype(BF16))
            xpair = xdt_g[:, j * LANES:(j + 1) * LANES]
            rhs = jnp.concatenate([jnp.where(lower_half, xpair, 0.0).astype(BF16),
                                   jnp.where(lower_half, 0.0, xpair).astype(BF16)], axis=0)
            ls = slice(grp * gw + j * LANES, grp * gw + (j + 1) * LANES)
            y_scr[:, ls] = y_g[:, j * LANES:(j + 1) * LANES] + _dot(jnp.concatenate(ws, axis=1), rhs)
        states_t = _dot_tn(b_g, (xdt_g * dec_g).astype(BF16))
        ht_scr[:, gs] = ht_g * ecs_g[q - 1:q, :] + states_t

    for grp in range(SSM_GROUPS):
        gs = slice(grp * gw, (grp + 1) * gw)
        yb_ref[:, gs] = _gate_groupnorm(y_scr[:, gs], z_ref[:, gs].astype(F32), ng_ref[:, gs]).astype(yb_ref.dtype)

    @pl.when(c == n_chunks - 1)
    def _():
        for grp in range(SSM_GROUPS):
            gs = slice(grp * gw, (grp + 1) * gw)
            h_ref[gs, :] = ht_scr[:, gs].T


def _ssd_prompt(xbc, z, dt, conv_w, conv_b, a_log, d_exp, norm_g, e2, batch, seq, depth, layer, prev):
    q = SSD_CHUNK
    n_chunks = seq // q
    conv_dim = xbc.shape[1]
    d_inner = z.shape[1]
    dtw = dt.shape[1]
    est = (2 * (q * conv_dim * 2 + q * d_inner * 2 + q * dtw * 4 + q * d_inner * 2 + d_inner * SSM_STATE * 4
                + 8 * conv_dim * 4 + 5 * conv_dim * 4 + 2 * d_inner * 4 + 2 * LANES * d_inner * 2)
           + (q + 16) * conv_dim * 4 + q * conv_dim * 4 + 2 * q * d_inner * 4 + (8 << 20))
    row_map = lambda b, c: (b * n_chunks + c, 0)
    const = lambda b, c: (0, 0)
    in_specs = [
        pl.BlockSpec((q, conv_dim), row_map),
        pl.BlockSpec((q, d_inner), row_map),
        pl.BlockSpec((q, dtw), row_map),
        _lspec((CONV_WIDTH, conv_dim), layer, const),
        _lspec((1, conv_dim), layer, const),
        _lspec((1, dtw), layer, const),
        _lspec((1, d_inner), layer, const),
        _lspec((1, d_inner), layer, const),
        pl.BlockSpec((2 * LANES, d_inner), const),
    ]
    args = [xbc, z, dt, conv_w, conv_b, a_log, d_exp, norm_g, e2]
    aliases = _chain(args, in_specs, prev, 1)
    return pl.pallas_call(
        functools.partial(_ssd_prompt_body, d_inner=d_inner, n_chunks=n_chunks, chained=prev is not None),
        grid=(batch, n_chunks),
        in_specs=in_specs,
        out_specs=[
            pl.BlockSpec((q, d_inner), row_map),
            pl.BlockSpec((None, None, d_inner, SSM_STATE), lambda b, c: (layer, b, 0, 0)),
            pl.BlockSpec((None, None, CONV_WIDTH - 1, conv_dim), lambda b, c: (layer, b, 0, 0)),
        ],
        out_shape=[
            jax.ShapeDtypeStruct((batch * seq, d_inner), BF16),
            jax.ShapeDtypeStruct((depth, batch, d_inner, SSM_STATE), F32),
            jax.ShapeDtypeStruct((depth, batch, CONV_WIDTH - 1, conv_dim), F32),
        ],
        scratch_shapes=[
            pltpu.VMEM((q + 2 * SUBLANES, conv_dim), F32),
            pltpu.VMEM((q, conv_dim), F32),
            pltpu.VMEM((q, d_inner), F32),
            pltpu.VMEM((SSM_STATE, d_inner), F32),
            pltpu.VMEM((LANES, q), F32),
        ],
        input_output_aliases=aliases,
        compiler_params=pltpu.CompilerParams(
            dimension_semantics=("arbitrary", "arbitrary"), vmem_limit_bytes=_vmem_limit(est)),
        name="ssd_prompt",
    )(*args)


def _ssd_sample_body(xbc_ref, z_ref, dte_ref, cbuf_ref, h0_ref, cw_ref, cb_ref, aexp_ref, dexp_ref, ng_ref,
                     *rest, d_inner, steps, chained):
    if chained:
        rest = rest[2:]
    yb_ref, h_ref, cst_ref, win_scr = rest
    q = xbc_ref.shape[0]
    nseq = q // steps
    conv_dim = xbc_ref.shape[1]
    n_state = SSM_STATE
    gw = d_inner // SSM_GROUPS
    halo = SUBLANES
    span = 2 * SUBLANES

    win_scr[...] = jnp.zeros(win_scr.shape, F32)
    for j in range(nseq):
        win_scr[j * span + halo - 3:j * span + halo, :] = cbuf_ref[j]
        win_scr[j * span + halo:j * span + halo + steps, :] = xbc_ref[j * steps:(j + 1) * steps, :]
        cst_ref[j] = win_scr[j * span + halo + steps - 3:j * span + halo + steps, :]
    rowc = lax.broadcasted_iota(jnp.int32, (q, conv_dim), 0)
    acc = None
    for j in range(nseq):
        base = j * span + halo - 3 - j * steps
        acc_j = cb_ref[...] + cw_ref[0:1, :] * win_scr[base:base + q, :]
        for k in range(1, CONV_WIDTH):
            acc_j = acc_j + cw_ref[k:k + 1, :] * win_scr[base + k:base + k + q, :]
        acc = acc_j if acc is None else jnp.where(rowc >= j * steps, acc_j, acc)
    act = _silu(acc)

    xs = act[:, :d_inner]
    row = lax.broadcasted_iota(jnp.int32, (q, d_inner), 0)
    tok = row % steps
    dt = dte_ref[...]
    da = dt * aexp_ref[...]
    cs = da
    s = 1
    while s < steps:
        cs = cs + jnp.where(tok >= s, pltpu.roll(cs, s, 0), 0.0)
        s *= 2
    cs_last = cs[steps - 1:steps, :]
    for j in range(1, nseq):
        cs_last = jnp.where(row >= j * steps, cs[(j + 1) * steps - 1:(j + 1) * steps, :], cs_last)
    xdt = xs * dt

    b_all = act[:, d_inner:d_inner + SSM_GROUPS * n_state]
    c_all = act[:, d_inner + SSM_GROUPS * n_state:]
    y = dexp_ref[...] * xs
    for o in range(steps):
        b_sh = b_all if o == 0 else pltpu.roll(b_all, o, 0)
        cs_sh = cs if o == 0 else pltpu.roll(cs, o, 0)
        xdt_sh = xdt if o == 0 else pltpu.roll(xdt, o, 0)
        prod = c_all * b_sh
        cb_o = jnp.concatenate(
            [jnp.broadcast_to(jnp.sum(prod[:, grp * n_state:(grp + 1) * n_state], axis=1, keepdims=True), (q, gw))
             for grp in range(SSM_GROUPS)], axis=1)
        keep = tok >= o
        lmat = jnp.where(keep, jnp.exp(jnp.where(keep, cs - cs_sh, 0.0)), 0.0)
        y = y + cb_o * lmat * xdt_sh

    ecs = jnp.exp(cs)
    xs_dec = xdt * jnp.exp(cs_last - cs)
    cd = jnp.exp(cs_last)
    rowg = lax.broadcasted_iota(jnp.int32, (q, gw), 0)
    lane = lax.broadcasted_iota(jnp.int32, (q, LANES), 1)
    p = SSM_HEAD_DIM
    for grp in range(SSM_GROUPS):
        gs = slice(grp * gw, (grp + 1) * gw)
        b_g = b_all[:, grp * n_state:(grp + 1) * n_state].astype(BF16)
        c_g = c_all[:, grp * n_state:(grp + 1) * n_state].astype(BF16)
        head_cd = []
        for t in range(gw // LANES):
            v = cd[:, grp * gw + t * LANES:grp * gw + (t + 1) * LANES]
            vr = pltpu.roll(v, p, 1)
            head_cd += [jnp.where(lane < p, v, vr), jnp.where(lane < p, vr, v)]
        y_off = None
        for j in range(nseq):
            h_g = h0_ref[j, gs, :]
            mine = (rowg >= j * steps) & (rowg < (j + 1) * steps)
            y_off_j = _dot_nt(c_g, h_g.astype(BF16))
            y_off = y_off_j if y_off is None else jnp.where(mine, y_off_j, y_off)
            states = _dot_tn(jnp.where(mine, xs_dec[:, gs], 0.0).astype(BF16), b_g)
            scale = jnp.concatenate(
                [jnp.broadcast_to(hc[j * steps:j * steps + 1, :], (p, n_state)) for hc in head_cd], axis=0)
            h_ref[j, gs, :] = h_g * scale + states
        y_g = y[:, gs] + y_off * ecs[:, gs]
        yb_ref[:, gs] = _gate_groupnorm(y_g, z_ref[:, gs].astype(F32), ng_ref[:, gs]).astype(yb_ref.dtype)


def _ssd_sample(xbc, z, dte, conv_state, h0_all, layer, conv_w, conv_b, a_exp, d_exp, norm_g, steps, prev):
    m, conv_dim = xbc.shape
    d_inner = z.shape[1]
    depth, nseq_total = h0_all.shape[:2]
    q = SUBLANES
    nseq = q // steps
    est = (2 * (2 * nseq * d_inner * SSM_STATE * 4 + q * (conv_dim + 3 * d_inner) * 4
                + 2 * nseq * SUBLANES * conv_dim * 4 + 5 * conv_dim * 4 + 3 * d_inner * 4)
           + 2 * nseq * SUBLANES * conv_dim * 4 + d_inner * LANES * 4 + (12 << 20))
    row_map = lambda s: (s, 0)
    lay_map = lambda s: (layer, s, 0, 0)
    const = lambda s: (0, 0)
    in_specs = [
        pl.BlockSpec((q, conv_dim), row_map),
        pl.BlockSpec((q, d_inner), row_map),
        pl.BlockSpec((q, d_inner), row_map),
        pl.BlockSpec((None, nseq, CONV_WIDTH - 1, conv_dim), lay_map),
        pl.BlockSpec((None, nseq, d_inner, SSM_STATE), lay_map),
        _lspec((CONV_WIDTH, conv_dim), layer, const),
        _lspec((1, conv_dim), layer, const),
        _lspec((1, d_inner), layer, const),
        _lspec((1, d_inner), layer, const),
        _lspec((1, d_inner), layer, const),
    ]
    args = [xbc, z, dte, conv_state, h0_all, conv_w, conv_b, a_exp, d_exp, norm_g]
    aliases = _chain(args, in_specs, prev, 1)
    return pl.pallas_call(
        functools.partial(_ssd_sample_body, d_inner=d_inner, steps=steps, chained=prev is not None),
        grid=(m // q,),
        in_specs=in_specs,
        out_specs=[
            pl.BlockSpec((q, d_inner), row_map),
            pl.BlockSpec((None, nseq, d_inner, SSM_STATE), lay_map),
            pl.BlockSpec((None, nseq, CONV_WIDTH - 1, conv_dim), lay_map),
        ],
        out_shape=[
            jax.ShapeDtypeStruct((m, d_inner), F32),
            jax.ShapeDtypeStruct((depth, nseq_total, d_inner, SSM_STATE), F32),
            jax.ShapeDtypeStruct((depth, nseq_total, CONV_WIDTH - 1, conv_dim), F32),
        ],
        scratch_shapes=[pltpu.VMEM((nseq * 2 * SUBLANES, conv_dim), F32)],
        input_output_aliases=aliases,
        compiler_params=pltpu.CompilerParams(
            dimension_semantics=("arbitrary",), vmem_limit_bytes=_vmem_limit(est)),
        name="ssd_sample",
    )(*args)


def _merge_body(a_ref, yb_ref, ga_ref, gb_ref, x_ref, wb_ref, wo_ref, g_ref, o_ref, h_ref):
    b = _dot(yb_ref[...].astype(BF16), wb_ref[...])
    merged = ga_ref[...].astype(F32) * a_ref[...].astype(F32) + gb_ref[...].astype(F32) * b
    x1 = x_ref[...] + _dot(merged.astype(BF16), wo_ref[...])
    o_ref[...] = x1
    h_ref[...] = _rmsnorm(x1, g_ref[...]).astype(h_ref.dtype)


def _merge(a, yb, gates, x, w_out_b, w_o, norm_g, layer, rows):
    m, d = x.shape
    d_inner = yb.shape[1]
    est = (2 * rows * (d * 2 + d_inner * yb.dtype.itemsize + 2 * d * gates.dtype.itemsize + 2 * d * 4 + d * 2)
           + d_inner * d * 2 + d * d * 2 + 3 * rows * d * 4)
    return pl.pallas_call(
        _merge_body,
        grid=(m // rows,),
        in_specs=[
            pl.BlockSpec((rows, d), lambda i: (i, 0)),
            pl.BlockSpec((rows, d_inner), lambda i: (i, 0)),
            pl.BlockSpec((rows, d), lambda i: (i, 0)),
            pl.BlockSpec((rows, d), lambda i: (i, 1)),
            pl.BlockSpec((rows, d), lambda i: (i, 0)),
            pl.BlockSpec((None, d_inner, d), lambda i: (layer, 0, 0), pipeline_mode=pl.Buffered(1)),
            pl.BlockSpec((None, d, d), lambda i: (layer, 0, 0), pipeline_mode=pl.Buffered(1)),
            _lspec((1, d), layer, lambda i: (0, 0)),
        ],
        out_specs=[pl.BlockSpec((rows, d), lambda i: (i, 0)), pl.BlockSpec((rows, d), lambda i: (i, 0))],
        out_shape=[jax.ShapeDtypeStruct((m, d), F32), jax.ShapeDtypeStruct((m, d), BF16)],
        compiler_params=pltpu.CompilerParams(
            dimension_semantics=("arbitrary",), vmem_limit_bytes=_vmem_limit(est)),
        name="merge",
    )(a, yb, gates, gates, x, w_out_b, w_o, norm_g)


def _ffn_body(x_ref, h_ref, wg_ref, wu_ref, wd_ref, g_ref, o_ref, *rest, n_f, final, sub):
    f = pl.program_id(1)
    blocks = _row_blocks(x_ref.shape[0], sub)

    @pl.when(f == 0)
    def _():
        for rs in blocks:
            o_ref[rs, :] = x_ref[rs, :]

    for rs in blocks:
        hb = h_ref[rs, :]
        act = _silu(_dot(hb, wg_ref[...])) * _dot(hb, wu_ref[...])
        o_ref[rs, :] += _dot(act.astype(BF16), wd_ref[...])

    @pl.when(f == n_f - 1)
    def _():
        for rs in blocks:
            y = _rmsnorm(o_ref[rs, :], g_ref[...])
            if final:
                o_ref[rs, :] = y
            else:
                rest[0][rs, :] = y.astype(BF16)


def _ffn(x, h, w_gate, w_up, w_down, next_g, layer, g_layer, tm, tf, final):
    m, d = x.shape
    hidden = w_gate.shape[2]
    n_f = hidden // tf
    est = 2 * (tm * d * 4 + tm * d * 2 + tm * d * 4 + tm * d * 2 + 3 * d * tf * 2) + 256 * (2 * tf + d) * 4
    out_specs = [pl.BlockSpec((tm, d), lambda i, j: (i, 0))]
    out_shape = [jax.ShapeDtypeStruct((m, d), F32)]
    if not final:
        out_specs.append(pl.BlockSpec((tm, d), lambda i, j: (i, 0)))
        out_shape.append(jax.ShapeDtypeStruct((m, d), BF16))
    return pl.pallas_call(
        functools.partial(_ffn_body, n_f=n_f, final=final, sub=256),
        grid=(m // tm, n_f),
        in_specs=[
            pl.BlockSpec((tm, d), lambda i, j: (i, 0)),
            pl.BlockSpec((tm, d), lambda i, j: (i, 0)),
            _lspec((d, tf), layer, lambda i, j: (0, j)),
            _lspec((d, tf), layer, lambda i, j: (0, j)),
            _lspec((tf, d), layer, lambda i, j: (j, 0)),
            _lspec((1, d), g_layer, lambda i, j: (0, 0)),
        ],
        out_specs=out_specs,
        out_shape=out_shape,
        compiler_params=pltpu.CompilerParams(
            dimension_semantics=("arbitrary", "arbitrary"), vmem_limit_bytes=_vmem_limit(est)),
        name="ffn",
    )(x, h, w_gate, w_up, w_down, next_g)


def _row_tile(m, target):
    t = min(m, target)
    while m % t:
        t //= 2
    return t


def kernel(x_prompt, x_sample, state_ssm, state_conv, norm1_g, w_in, conv_w, conv_b, dt_bias, a_log, d_skip, ssm_norm_g, sgu_ln_g, sgu_ln_b, sgu_w, sgu_b, w_out_a, w_out_b, w_o, norm2_g, w_ffn_gate, w_ffn_up, w_ffn_down, final_norm_g):
    batch, seq, d = x_prompt.shape
    nseq, steps, _ = x_sample.shape
    depth = w_in.shape[0]
    heads = dt_bias.shape[1]
    d_inner = heads * SSM_HEAD_DIM
    conv_dim = conv_w.shape[2]
    sgu_width = sgu_ln_g.shape[1]
    ngrp = sgu_w.shape[1]
    seg_widths = (2 * sgu_width, d_inner, conv_dim, 2 * d)
    dt_col = 2 * sgu_width + d_inner + conv_dim

    xp = x_prompt.reshape(batch * seq, d)
    xs = x_sample.reshape(nseq * steps, d)
    state_ssm_flat = state_ssm.reshape(depth, nseq, d_inner, SSM_STATE)

    def rows3(p):
        return p.reshape(depth, 1, p.shape[-1])

    w_gates_b = w_in[:, :, dt_col + heads:].astype(BF16)
    dt_b = rows3(jnp.pad(dt_bias, ((0, 0), (0, LANES - heads))))
    w_dt_exp = jnp.repeat(w_in[:, :, dt_col:dt_col + heads], SSM_HEAD_DIM, axis=2).astype(BF16)
    dt_b_exp = rows3(jnp.repeat(dt_bias, SSM_HEAD_DIM, axis=1))
    a_log_pad = rows3(jnp.pad(a_log, ((0, 0), (0, LANES - heads))))
    a_exp = rows3(jnp.repeat(-jnp.exp(a_log), SSM_HEAD_DIM, axis=1))
    d_exp = rows3(jnp.repeat(d_skip, SSM_HEAD_DIM, axis=1))
    n1, n2, ng = rows3(norm1_g), rows3(norm2_g), rows3(ssm_norm_g)
    cb, lng, lnb = rows3(conv_b), rows3(sgu_ln_g), rows3(sgu_ln_b)
    tril = jnp.tril(jnp.ones((SGU_CHUNK, SGU_CHUNK), bool))
    mix_p = jnp.where(tril, sgu_w, 0).astype(BF16)
    bias_p = jnp.repeat(jnp.swapaxes(sgu_b, 1, 2), SGU_GROUP_DIM, axis=2)
    seqs_per_chunk = SGU_CHUNK // steps
    eye = jnp.eye(seqs_per_chunk, dtype=F32)
    w_small = jnp.where(jnp.tril(jnp.ones((steps, steps), bool)), sgu_w[:, :, :steps, :steps], 0)
    mix_s = jnp.einsum("ab,lgts->lgatbs", eye, w_small).reshape(depth, ngrp, SGU_CHUNK, SGU_CHUNK).astype(BF16)
    bias_s = jnp.tile(bias_p[:, :steps], (1, seqs_per_chunk, 1))
    wa, wb, wo = w_out_a.astype(BF16), w_out_b.astype(BF16), w_o.astype(BF16)
    wg, wu, wd = w_ffn_gate.astype(BF16), w_ffn_up.astype(BF16), w_ffn_down.astype(BF16)
    expand = (jnp.arange(d_inner)[None, :] // SSM_HEAD_DIM == jnp.arange(LANES)[:, None]).astype(BF16)
    e2 = jnp.concatenate([expand, expand], axis=0)

    fin_g3 = final_norm_g.reshape(1, 1, d)
    w_su, w_z, w_xbc, w_gt = seg_widths

    def project(h, l, tm, dtype):
        su = _proj(h, w_in, l, 0, w_su, jax.nn.gelu, dtype, tm, 1024)
        z = _proj(h, w_in, l, w_su, w_z, _identity, dtype, tm, 1024)
        xbc = _proj(h, w_in, l, w_su + w_z, w_xbc, _identity, dtype, tm, 1024)
        gt = _proj(h, w_gates_b, l, 0, w_gt, jax.nn.sigmoid, dtype, tm, 1024)
        return su, z, xbc, gt

    def finish(x, a, yb, gt, l, tm):
        x, h2 = _merge(a, yb, gt, x, wb, wo, n2, l, _row_tile(x.shape[0], 256))
        if l == depth - 1:
            (y,) = _ffn(x, h2, wg, wu, wd, fin_g3, l, 0, tm, 512, True)
            return y, None
        return _ffn(x, h2, wg, wu, wd, n1, l, l + 1, tm, 512, False)

    tm_p = _row_tile(xp.shape[0], 1024)
    tm_s = _row_tile(xs.shape[0], 512)
    hp = _norm(xp, n1, 0, tm_p)
    hs = _norm(xs, n1, 0, tm_s)
    prev_p = prev_s = prev_v = None
    for l in range(depth):
        su, z, xbc, gt = project(hp, l, tm_p, BF16)
        dt = _proj(hp, w_in, l, dt_col, LANES, _softplus, F32, tm_p, LANES, bias=dt_b)
        (a,) = _sgu(su, lng, lnb, mix_p, bias_p, wa, l, _row_tile(xp.shape[0], 512), False)
        yb, *prev_p = _ssd_prompt(xbc, z, dt, conv_w, cb, a_log_pad, d_exp, ng, e2, batch, seq, depth, l, prev_p)
        xp, hp = finish(xp, a, yb, gt, l, _row_tile(xp.shape[0], 512))

        su, z, xbc, gt = project(hs, l, tm_s, F32)
        dte = _proj(hs, w_dt_exp, l, 0, d_inner, _softplus, F32, tm_s, 1024, bias=dt_b_exp)
        a, *prev_v = _sgu(su, lng, lnb, mix_s, bias_s, wa, l, tm_s, True, prev_v)
        yb, *prev_s = _ssd_sample(xbc, z, dte, state_conv, state_ssm_flat, l, conv_w, cb, a_exp, d_exp, ng,
                                  steps, prev_s)
        xs, hs = finish(xs, a, yb, gt, l, tm_s)

    ssm_p, conv_p = prev_p
    ssm_s, conv_s = prev_s
    (v_s,) = prev_v
    return (xp.reshape(batch, seq, d), xs.reshape(nseq, steps, d),
            ssm_p.reshape(depth, batch, heads, SSM_HEAD_DIM, SSM_STATE), conv_p,
            ssm_s.reshape(depth, nseq, heads, SSM_HEAD_DIM, SSM_STATE), conv_s,
            v_s.reshape(depth, nseq, steps, sgu_width))
```

```python
import functools

import jax
import jax.numpy as jnp
from jax import lax
from jax.experimental import pallas as pl
from jax.experimental.pallas import tpu as pltpu

F32 = jnp.float32
BF16 = jnp.bfloat16
EPS = 1e-6
LOG2_E = 1.4426950408889634

LANES = 128
SUBLANES = 8
VMEM_CAP_BYTES = 64 * 1024 * 1024

SGU_CHUNK = 128
SGU_GROUP_DIM = 128
SSM_HEAD_DIM = 64
SSM_STATE = 128
SSM_GROUPS = 8
CONV_WIDTH = 4
CONV_HALO = 16
SSD_CHUNK = 128


def _vmem_limit(estimate_bytes):
    return int(min(estimate_bytes + (8 << 20), VMEM_CAP_BYTES - (6 << 20)))


def _silu(x):
    return x * jax.nn.sigmoid(x)


def _softplus(x):
    return jnp.maximum(x, 0.0) + jnp.log1p(jnp.exp(-jnp.abs(x)))


def _rmsnorm(x, g):
    return x * lax.rsqrt(jnp.mean(x * x, axis=-1, keepdims=True) + EPS) * g


def _dot(a, b):
    return jnp.dot(a, b, preferred_element_type=F32)


def _dot_nt(a, b):
    return lax.dot_general(a, b, (((1,), (1,)), ((), ())), preferred_element_type=F32)


def _dot_tn(a, b):
    return lax.dot_general(a, b, (((0,), (0,)), ((), ())), preferred_element_type=F32)


def _chain(args, in_specs, prev, first_out):
    aliases = {}
    if prev is not None:
        for k, arr in enumerate(prev):
            aliases[len(args)] = first_out + k
            args.append(arr)
            in_specs.append(pl.BlockSpec(memory_space=pl.ANY))
    return aliases


def _row_blocks(rows, sub):
    sub = min(rows, sub)
    return [slice(r * sub, (r + 1) * sub) for r in range(rows // sub)]


def _lspec(block, layer, tail_index):
    return pl.BlockSpec((None,) + tuple(block), lambda *ids: (layer,) + tuple(tail_index(*ids)))


def _norm_body(x_ref, g_ref, o_ref):
    o_ref[...] = _rmsnorm(x_ref[...], g_ref[...]).astype(o_ref.dtype)


def _norm(x, g, layer, rows):
    m, d = x.shape
    est = 2 * rows * d * 6 + rows * d * 8
    return pl.pallas_call(
        _norm_body,
        grid=(m // rows,),
        in_specs=[pl.BlockSpec((rows, d), lambda i: (i, 0)), _lspec((1, d), layer, lambda i: (0, 0))],
        out_specs=pl.BlockSpec((rows, d), lambda i: (i, 0)),
        out_shape=jax.ShapeDtypeStruct((m, d), BF16),
        compiler_params=pltpu.CompilerParams(
            dimension_semantics=("arbitrary",), vmem_limit_bytes=_vmem_limit(est)),
        name="norm",
    )(x, g)


def _proj_body(h_ref, w_ref, *rest, act, sub, cast, biased):
    rest = list(rest)
    b_ref = rest.pop(0) if biased else None
    o_ref = rest.pop(0)
    if cast:
        (w_bf,) = rest

        @pl.when(pl.program_id(1) == 0)
        def _():
            w_bf[...] = w_ref[0].astype(BF16)
    else:
        w_bf = w_ref.at[0]
    for rs in _row_blocks(h_ref.shape[0], sub):
        t = _dot_nt(h_ref[rs, :], w_bf[...])
        if biased:
            t = t + b_ref[...]
        o_ref[rs, :] = act(t).astype(o_ref.dtype)


def _proj(h, w_t, layer, col0, width, act, out_dtype, tm, tn, bias=None):
    m, d = h.shape
    cast = w_t.dtype != BF16
    osz = jnp.dtype(out_dtype).itemsize
    est = 2 * (tm * d * 2 + d * tn * w_t.dtype.itemsize + tm * tn * osz + tn * 4) + d * tn * 2 + 256 * tn * 8
    in_specs = [
        pl.BlockSpec((tm, d), lambda j, i: (i, 0)),
        pl.BlockSpec((pl.Element(1), pl.Element(tn), pl.Element(d)),
                     lambda j, i: (layer, pl.multiple_of(col0 + j * tn, SUBLANES), 0)),
    ]
    args = [h, w_t]
    if bias is not None:
        in_specs.append(_lspec((1, tn), layer, lambda j, i: (0, j)))
        args.append(bias)
    return pl.pallas_call(
        functools.partial(_proj_body, act=act, sub=256, cast=cast, biased=bias is not None),
        grid=(width // tn, m // tm),
        in_specs=in_specs,
        out_specs=pl.BlockSpec((tm, tn), lambda j, i: (i, j)),
        out_shape=jax.ShapeDtypeStruct((m, width), out_dtype),
        scratch_shapes=[pltpu.VMEM((tn, d), BF16)] if cast else [],
        compiler_params=pltpu.CompilerParams(
            dimension_semantics=("arbitrary", "arbitrary"), vmem_limit_bytes=_vmem_limit(est)),
        name="proj",
    )(*args)


def _identity(t):
    return t


def _sgu_body(u_ref, v_ref, lng_ref, lnb_ref, mix_ref, bias_ref, wa_ref, *rest, emit_v, chained):
    if emit_v:
        if chained:
            rest = rest[1:]
        a_ref, vout_ref, ya_scr = rest
    else:
        a_ref, ya_scr = rest
    vg = v_ref[...].astype(F32)
    mu = jnp.mean(vg, axis=-1, keepdims=True)
    var = jnp.mean(jnp.square(vg - mu), axis=-1, keepdims=True)
    v = (vg - mu) * lax.rsqrt(var + EPS) * lng_ref[...] + lnb_ref[...]
    if emit_v:
        vout_ref[...] = v
    vb = v.astype(BF16)
    rows, width = vb.shape
    for c in range(rows // SGU_CHUNK):
        rs = slice(c * SGU_CHUNK, (c + 1) * SGU_CHUNK)
        for grp in range(width // SGU_GROUP_DIM):
            cs = slice(grp * SGU_GROUP_DIM, (grp + 1) * SGU_GROUP_DIM)
            mixed = _dot(mix_ref[grp], vb[rs, cs]) + bias_ref[:, cs]
            ya_scr[rs, cs] = (u_ref[rs, cs].astype(F32) * mixed).astype(BF16)
    a_ref[...] = _dot(ya_scr[...], wa_ref[...]).astype(a_ref.dtype)


def _sgu(su, lng, lnb, mix, bias, w_out_a, layer, rows, emit_v, prev=None):
    m = su.shape[0]
    width = su.shape[1] // 2
    depth, _, d = w_out_a.shape
    ngrp = mix.shape[1]
    isz = su.dtype.itemsize
    est = 2 * (2 * rows * width * isz + ngrp * SGU_CHUNK * SGU_CHUNK * 2 + SGU_CHUNK * width * 4
               + rows * d * 2 + rows * width * 4) + width * d * 2 + 4 * rows * width * 4
    out_shape = [jax.ShapeDtypeStruct((m, d), BF16)]
    out_specs = [pl.BlockSpec((rows, d), lambda i: (i, 0))]
    if emit_v:
        out_shape.append(jax.ShapeDtypeStruct((depth, m, width), F32))
        out_specs.append(pl.BlockSpec((None, rows, width), lambda i: (layer, i, 0)))
    in_specs = [
        pl.BlockSpec((rows, width), lambda i: (i, 0)),
        pl.BlockSpec((rows, width), lambda i: (i, 1)),
        _lspec((1, width), layer, lambda i: (0, 0)),
        _lspec((1, width), layer, lambda i: (0, 0)),
        _lspec((ngrp, SGU_CHUNK, SGU_CHUNK), layer, lambda i: (0, 0, 0)),
        _lspec((SGU_CHUNK, width), layer, lambda i: (0, 0)),
        pl.BlockSpec((None, width, d), lambda i: (layer, 0, 0), pipeline_mode=pl.Buffered(1)),
    ]
    args = [su, su, lng, lnb, mix, bias, w_out_a]
    aliases = _chain(args, in_specs, prev, 1)
    return pl.pallas_call(
        functools.partial(_sgu_body, emit_v=emit_v, chained=prev is not None),
        grid=(m // rows,),
        in_specs=in_specs,
        out_specs=out_specs,
        out_shape=out_shape,
        scratch_shapes=[pltpu.VMEM((rows, width), BF16)],
        input_output_aliases=aliases,
        compiler_params=pltpu.CompilerParams(
            dimension_semantics=("arbitrary",), vmem_limit_bytes=_vmem_limit(est)),
        name="sgu",
    )(*args)


def _prefix_sum_rows(x, row):
    n = x.shape[0]
    s = 1
    while s < n:
        x = x + jnp.where(row >= s, pltpu.roll(x, s, 0), 0.0)
        s *= 2
    return x


def _gate_groupnorm(y, z, ng):
    t = y * _silu(z)
    return t * lax.rsqrt(jnp.mean(t * t, axis=-1, keepdims=True) + EPS) * ng


def _split_hi_lo(v):
    hi = v.astype(BF16)
    lo = (v - hi.astype(F32)).astype(BF16)
    return jnp.concatenate([hi, lo], axis=1)


def _ssd_prompt_body(xbc_ref, z_ref, dt_ref, cw_ref, cb_ref, alog_ref, dexp_ref, ng_ref, e2_ref, shift_ref, *rest,
                     d_inner, n_chunks, chained):
    if chained:
        rest = rest[2:]
    yb_ref, h_ref, cst_ref, win_scr, act_scr, y_scr, ht_scr, cst_scr = rest
    c = pl.program_id(1)
    q = xbc_ref.shape[0]
    conv_dim = xbc_ref.shape[1]
    p = SSM_HEAD_DIM
    n_state = SSM_STATE
    gw = d_inner // SSM_GROUPS
    pairs_per_group = gw // (2 * p)
    halo = CONV_HALO

    @pl.when(c == 0)
    def _():
        win_scr[...] = jnp.zeros(win_scr.shape, BF16)
        ht_scr[...] = jnp.zeros(ht_scr.shape, F32)

    win_scr[halo:halo + q, :] = xbc_ref[...]
    lane_chunk = 512
    for j in range(conv_dim // lane_chunk):
        ls = slice(j * lane_chunk, (j + 1) * lane_chunk)
        shifted = _dot(shift_ref[...], win_scr[:, ls])
        acc = cb_ref[:, ls] + cw_ref[CONV_WIDTH - 1:CONV_WIDTH, ls] * xbc_ref[:, ls].astype(F32)
        for k in range(CONV_WIDTH - 1):
            acc = acc + cw_ref[k:k + 1, ls] * shifted[k * q:(k + 1) * q, :]
        act_scr[:, ls] = _silu(acc)

    @pl.when(c == n_chunks - 1)
    def _():
        tail = xbc_ref[q - halo:q, :].astype(F32)
        cst_ref[...] = tail[halo - (CONV_WIDTH - 1):halo, :]

    win_scr[0:halo, :] = xbc_ref[q - halo:q, :]

    row = lax.broadcasted_iota(jnp.int32, (q, LANES), 0)
    lane = lax.broadcasted_iota(jnp.int32, (q, LANES), 1)
    dt = dt_ref[...]
    da = dt * (-jnp.exp(alog_ref[...]))
    cs = _prefix_sum_rows(da, row)
    cs2 = cs * LOG2_E
    cst_scr[...] = cs2.T
    dt_hl = _split_hi_lo(dt)
    ecs_hl = _split_hi_lo(jnp.exp(cs))
    dec_hl = _split_hi_lo(jnp.exp(cs[q - 1:q, :] - cs))
    causal = row >= lane
    lower_half = lane < p

    for grp in range(SSM_GROUPS):
        gs = slice(grp * gw, (grp + 1) * gw)
        e2 = e2_ref[:, gs]
        dt_g = _dot(dt_hl, e2)
        ecs_g = _dot(ecs_hl, e2)
        dec_g = _dot(dec_hl, e2)
        xs_g = act_scr[:, gs]
        xdt_g = xs_g * dt_g
        b_g = act_scr[:, d_inner + grp * n_state:d_inner + (grp + 1) * n_state].astype(BF16)
        c_g = act_scr[:, d_inner + (SSM_GROUPS + grp) * n_state:
                      d_inner + (SSM_GROUPS + grp + 1) * n_state].astype(BF16)
        cbm = jnp.where(causal, _dot_nt(c_g, b_g), 0.0)
        ht_g = ht_scr[:, gs]
        y_g = _dot(c_g, ht_g.astype(BF16)) * ecs_g + dexp_ref[:, gs] * xs_g
        for j in range(pairs_per_group):
            h0 = grp * 2 * pairs_per_group + 2 * j
            ws = []
            for h in (h0, h0 + 1):
                seg2 = cs2[:, h:h + 1] - cst_scr[h:h + 1, :]
                ws.append((cbm * jnp.exp2(jnp.where(causal, seg2, 0.0))).astype(BF16))
            xpair = xdt_g[:, j * LANES:(j + 1) * LANES]
            rhs = jnp.concatenate([jnp.where(lower_half, xpair, 0.0).astype(BF16),
                                   jnp.where(lower_half, 0.0, xpair).astype(BF16)], axis=0)
            ls = slice(grp * gw + j * LANES, grp * gw + (j + 1) * LANES)
            y_scr[:, ls] = y_g[:, j * LANES:(j + 1) * LANES] + _dot(jnp.concatenate(ws, axis=1), rhs)
        states_t = _dot_tn(b_g, (xdt_g * dec_g).astype(BF16))
        ht_scr[:, gs] = ht_g * ecs_g[q - 1:q, :] + states_t

    for grp in range(SSM_GROUPS):
        gs = slice(grp * gw, (grp + 1) * gw)
        yb_ref[:, gs] = _gate_groupnorm(y_scr[:, gs], z_ref[:, gs].astype(F32), ng_ref[:, gs]).astype(yb_ref.dtype)

    @pl.when(c == n_chunks - 1)
    def _():
        for grp in range(SSM_GROUPS):
            gs = slice(grp * gw, (grp + 1) * gw)
            h_ref[gs, :] = ht_scr[:, gs].T


def _ssd_prompt(xbc, z, dt, conv_w, conv_b, a_log, d_exp, norm_g, e2, shift, batch, seq, depth, layer, prev):
    q = SSD_CHUNK
    n_chunks = seq // q
    conv_dim = xbc.shape[1]
    d_inner = z.shape[1]
    dtw = dt.shape[1]
    est = (2 * (q * conv_dim * 2 + q * d_inner * 2 + q * dtw * 4 + q * d_inner * 2 + d_inner * SSM_STATE * 4
                + 8 * conv_dim * 4 + 5 * conv_dim * 4 + 2 * d_inner * 4 + 2 * LANES * d_inner * 2)
           + (q + 16) * conv_dim * 4 + q * conv_dim * 4 + 2 * q * d_inner * 4 + (8 << 20))
    row_map = lambda b, c: (b * n_chunks + c, 0)
    const = lambda b, c: (0, 0)
    in_specs = [
        pl.BlockSpec((q, conv_dim), row_map),
        pl.BlockSpec((q, d_inner), row_map),
        pl.BlockSpec((q, dtw), row_map),
        _lspec((CONV_WIDTH, conv_dim), layer, const),
        _lspec((1, conv_dim), layer, const),
        _lspec((1, dtw), layer, const),
        _lspec((1, d_inner), layer, const),
        _lspec((1, d_inner), layer, const),
        pl.BlockSpec((2 * LANES, d_inner), const),
        pl.BlockSpec(shift.shape, const),
    ]
    args = [xbc, z, dt, conv_w, conv_b, a_log, d_exp, norm_g, e2, shift]
    aliases = _chain(args, in_specs, prev, 1)
    return pl.pallas_call(
        functools.partial(_ssd_prompt_body, d_inner=d_inner, n_chunks=n_chunks, chained=prev is not None),
        grid=(batch, n_chunks),
        in_specs=in_specs,
        out_specs=[
            pl.BlockSpec((q, d_inner), row_map),
            pl.BlockSpec((None, None, d_inner, SSM_STATE), lambda b, c: (layer, b, 0, 0)),
            pl.BlockSpec((None, None, CONV_WIDTH - 1, conv_dim), lambda b, c: (layer, b, 0, 0)),
        ],
        out_shape=[
            jax.ShapeDtypeStruct((batch * seq, d_inner), BF16),
            jax.ShapeDtypeStruct((depth, batch, d_inner, SSM_STATE), F32),
            jax.ShapeDtypeStruct((depth, batch, CONV_WIDTH - 1, conv_dim), F32),
        ],
        scratch_shapes=[
            pltpu.VMEM((2 * q, conv_dim), BF16),
            pltpu.VMEM((q, conv_dim), F32),
            pltpu.VMEM((q, d_inner), F32),
            pltpu.VMEM((SSM_STATE, d_inner), F32),
            pltpu.VMEM((LANES, q), F32),
        ],
        input_output_aliases=aliases,
        compiler_params=pltpu.CompilerParams(
            dimension_semantics=("arbitrary", "arbitrary"), vmem_limit_bytes=_vmem_limit(est)),
        name="ssd_prompt",
    )(*args)


def _ssd_sample_body(xbc_ref, z_ref, dte_ref, cbuf_ref, h0_ref, cw_ref, cb_ref, aexp_ref, dexp_ref, ng_ref,
                     *rest, d_inner, steps, chained):
    if chained:
        rest = rest[2:]
    yb_ref, h_ref, cst_ref, win_scr = rest
    q = xbc_ref.shape[0]
    nseq = q // steps
    conv_dim = xbc_ref.shape[1]
    n_state = SSM_STATE
    gw = d_inner // SSM_GROUPS
    halo = SUBLANES
    span = 2 * SUBLANES

    win_scr[...] = jnp.zeros(win_scr.shape, F32)
    for j in range(nseq):
        win_scr[j * span + halo - 3:j * span + halo, :] = cbuf_ref[j]
        win_scr[j * span + halo:j * span + halo + steps, :] = xbc_ref[j * steps:(j + 1) * steps, :]
        cst_ref[j] = win_scr[j * span + halo + steps - 3:j * span + halo + steps, :]
    rowc = lax.broadcasted_iota(jnp.int32, (q, conv_dim), 0)
    acc = None
    for j in range(nseq):
        base = j * span + halo - 3 - j * steps
        acc_j = cb_ref[...] + cw_ref[0:1, :] * win_scr[base:base + q, :]
        for k in range(1, CONV_WIDTH):
            acc_j = acc_j + cw_ref[k:k + 1, :] * win_scr[base + k:base + k + q, :]
        acc = acc_j if acc is None else jnp.where(rowc >= j * steps, acc_j, acc)
    act = _silu(acc)

    xs = act[:, :d_inner]
    row = lax.broadcasted_iota(jnp.int32, (q, d_inner), 0)
    tok = row % steps
    dt = dte_ref[...]
    da = dt * aexp_ref[...]
    cs = da
    s = 1
    while s < steps:
        cs = cs + jnp.where(tok >= s, pltpu.roll(cs, s, 0), 0.0)
        s *= 2
    cs_last = cs[steps - 1:steps, :]
    for j in range(1, nseq):
        cs_last = jnp.where(row >= j * steps, cs[(j + 1) * steps - 1:(j + 1) * steps, :], cs_last)
    xdt = xs * dt

    b_all = act[:, d_inner:d_inner + SSM_GROUPS * n_state]
    c_all = act[:, d_inner + SSM_GROUPS * n_state:]
    y = dexp_ref[...] * xs
    for o in range(steps):
        b_sh = b_all if o == 0 else pltpu.roll(b_all, o, 0)
        cs_sh = cs if o == 0 else pltpu.roll(cs, o, 0)
        xdt_sh = xdt if o == 0 else pltpu.roll(xdt, o, 0)
        prod = c_all * b_sh
        cb_o = jnp.concatenate(
            [jnp.broadcast_to(jnp.sum(prod[:, grp * n_state:(grp + 1) * n_state], axis=1, keepdims=True), (q, gw))
             for grp in range(SSM_GROUPS)], axis=1)
        keep = tok >= o
        lmat = jnp.where(keep, jnp.exp(jnp.where(keep, cs - cs_sh, 0.0)), 0.0)
        y = y + cb_o * lmat * xdt_sh

    ecs = jnp.exp(cs)
    xs_dec = xdt * jnp.exp(cs_last - cs)
    cd = jnp.exp(cs_last)
    rowg = lax.broadcasted_iota(jnp.int32, (q, gw), 0)
    lane = lax.broadcasted_iota(jnp.int32, (q, LANES), 1)
    p = SSM_HEAD_DIM
    for grp in range(SSM_GROUPS):
        gs = slice(grp * gw, (grp + 1) * gw)
        b_g = b_all[:, grp * n_state:(grp + 1) * n_state].astype(BF16)
        c_g = c_all[:, grp * n_state:(grp + 1) * n_state].astype(BF16)
        head_cd = []
        for t in range(gw // LANES):
            v = cd[:, grp * gw + t * LANES:grp * gw + (t + 1) * LANES]
            vr = pltpu.roll(v, p, 1)
            head_cd += [jnp.where(lane < p, v, vr), jnp.where(lane < p, vr, v)]
        y_off = None
        for j in range(nseq):
            h_g = h0_ref[j, gs, :]
            mine = (rowg >= j * steps) & (rowg < (j + 1) * steps)
            y_off_j = _dot_nt(c_g, h_g.astype(BF16))
            y_off = y_off_j if y_off is None else jnp.where(mine, y_off_j, y_off)
            states = _dot_tn(jnp.where(mine, xs_dec[:, gs], 0.0).astype(BF16), b_g)
            scale = jnp.concatenate(
                [jnp.broadcast_to(hc[j * steps:j * steps + 1, :], (p, n_state)) for hc in head_cd], axis=0)
            h_ref[j, gs, :] = h_g * scale + states
        y_g = y[:, gs] + y_off * ecs[:, gs]
        yb_ref[:, gs] = _gate_groupnorm(y_g, z_ref[:, gs].astype(F32), ng_ref[:, gs]).astype(yb_ref.dtype)


def _ssd_sample(xbc, z, dte, conv_state, h0_all, layer, conv_w, conv_b, a_exp, d_exp, norm_g, steps, prev):
    m, conv_dim = xbc.shape
    d_inner = z.shape[1]
    depth, nseq_total = h0_all.shape[:2]
    q = SUBLANES
    nseq = q // steps
    est = (2 * (2 * nseq * d_inner * SSM_STATE * 4 + q * (conv_dim + 3 * d_inner) * 4
                + 2 * nseq * SUBLANES * conv_dim * 4 + 5 * conv_dim * 4 + 3 * d_inner * 4)
           + 2 * nseq * SUBLANES * conv_dim * 4 + d_inner * LANES * 4 + (12 << 20))
    row_map = lambda s: (s, 0)
    lay_map = lambda s: (layer, s, 0, 0)
    const = lambda s: (0, 0)
    in_specs = [
        pl.BlockSpec((q, conv_dim), row_map),
        pl.BlockSpec((q, d_inner), row_map),
        pl.BlockSpec((q, d_inner), row_map),
        pl.BlockSpec((None, nseq, CONV_WIDTH - 1, conv_dim), lay_map),
        pl.BlockSpec((None, nseq, d_inner, SSM_STATE), lay_map),
        _lspec((CONV_WIDTH, conv_dim), layer, const),
        _lspec((1, conv_dim), layer, const),
        _lspec((1, d_inner), layer, const),
        _lspec((1, d_inner), layer, const),
        _lspec((1, d_inner), layer, const),
    ]
    args = [xbc, z, dte, conv_state, h0_all, conv_w, conv_b, a_exp, d_exp, norm_g]
    aliases = _chain(args, in_specs, prev, 1)
    return pl.pallas_call(
        functools.partial(_ssd_sample_body, d_inner=d_inner, steps=steps, chained=prev is not None),
        grid=(m // q,),
        in_specs=in_specs,
        out_specs=[
            pl.BlockSpec((q, d_inner), row_map),
            pl.BlockSpec((None, nseq, d_inner, SSM_STATE), lay_map),
            pl.BlockSpec((None, nseq, CONV_WIDTH - 1, conv_dim), lay_map),
        ],
        out_shape=[
            jax.ShapeDtypeStruct((m, d_inner), F32),
            jax.ShapeDtypeStruct((depth, nseq_total, d_inner, SSM_STATE), F32),
            jax.ShapeDtypeStruct((depth, nseq_total, CONV_WIDTH - 1, conv_dim), F32),
        ],
        scratch_shapes=[pltpu.VMEM((nseq * 2 * SUBLANES, conv_dim), F32)],
        input_output_aliases=aliases,
        compiler_params=pltpu.CompilerParams(
            dimension_semantics=("arbitrary",), vmem_limit_bytes=_vmem_limit(est)),
        name="ssd_sample",
    )(*args)


def _merge_body(a_ref, yb_ref, ga_ref, gb_ref, x_ref, wb_ref, wo_ref, g_ref, o_ref, h_ref):
    b = _dot(yb_ref[...].astype(BF16), wb_ref[...])
    merged = ga_ref[...].astype(F32) * a_ref[...].astype(F32) + gb_ref[...].astype(F32) * b
    x1 = x_ref[...] + _dot(merged.astype(BF16), wo_ref[...])
    o_ref[...] = x1
    h_ref[...] = _rmsnorm(x1, g_ref[...]).astype(h_ref.dtype)


def _merge(a, yb, gates, x, w_out_b, w_o, norm_g, layer, rows):
    m, d = x.shape
    d_inner = yb.shape[1]
    est = (2 * rows * (d * 2 + d_inner * yb.dtype.itemsize + 2 * d * gates.dtype.itemsize + 2 * d * 4 + d * 2)
           + d_inner * d * 2 + d * d * 2 + 3 * rows * d * 4)
    return pl.pallas_call(
        _merge_body,
        grid=(m // rows,),
        in_specs=[
            pl.BlockSpec((rows, d), lambda i: (i, 0)),
            pl.BlockSpec((rows, d_inner), lambda i: (i, 0)),
            pl.BlockSpec((rows, d), lambda i: (i, 0)),
            pl.BlockSpec((rows, d), lambda i: (i, 1)),
            pl.BlockSpec((rows, d), lambda i: (i, 0)),
            pl.BlockSpec((None, d_inner, d), lambda i: (layer, 0, 0), pipeline_mode=pl.Buffered(1)),
            pl.BlockSpec((None, d, d), lambda i: (layer, 0, 0), pipeline_mode=pl.Buffered(1)),
            _lspec((1, d), layer, lambda i: (0, 0)),
        ],
        out_specs=[pl.BlockSpec((rows, d), lambda i: (i, 0)), pl.BlockSpec((rows, d), lambda i: (i, 0))],
        out_shape=[jax.ShapeDtypeStruct((m, d), F32), jax.ShapeDtypeStruct((m, d), BF16)],
        compiler_params=pltpu.CompilerParams(
            dimension_semantics=("arbitrary",), vmem_limit_bytes=_vmem_limit(est)),
        name="merge",
    )(a, yb, gates, gates, x, w_out_b, w_o, norm_g)


def _ffn_body(x_ref, h_ref, wg_ref, wu_ref, wd_ref, g_ref, o_ref, *rest, n_f, final, sub):
    f = pl.program_id(1)
    blocks = _row_blocks(x_ref.shape[0], sub)

    @pl.when(f == 0)
    def _():
        for rs in blocks:
            o_ref[rs, :] = x_ref[rs, :]

    for rs in blocks:
        hb = h_ref[rs, :]
        act = _silu(_dot(hb, wg_ref[...])) * _dot(hb, wu_ref[...])
        o_ref[rs, :] += _dot(act.astype(BF16), wd_ref[...])

    @pl.when(f == n_f - 1)
    def _():
        for rs in blocks:
            y = _rmsnorm(o_ref[rs, :], g_ref[...])
            if final:
                o_ref[rs, :] = y
            else:
                rest[0][rs, :] = y.astype(BF16)


def _ffn(x, h, w_gate, w_up, w_down, next_g, layer, g_layer, tm, tf, final):
    m, d = x.shape
    hidden = w_gate.shape[2]
    n_f = hidden // tf
    est = 2 * (tm * d * 4 + tm * d * 2 + tm * d * 4 + tm * d * 2 + 3 * d * tf * 2) + 256 * (2 * tf + d) * 4
    out_specs = [pl.BlockSpec((tm, d), lambda i, j: (i, 0))]
    out_shape = [jax.ShapeDtypeStruct((m, d), F32)]
    if not final:
        out_specs.append(pl.BlockSpec((tm, d), lambda i, j: (i, 0)))
        out_shape.append(jax.ShapeDtypeStruct((m, d), BF16))
    return pl.pallas_call(
        functools.partial(_ffn_body, n_f=n_f, final=final, sub=256),
        grid=(m // tm, n_f),
        in_specs=[
            pl.BlockSpec((tm, d), lambda i, j: (i, 0)),
            pl.BlockSpec((tm, d), lambda i, j: (i, 0)),
            _lspec((d, tf), layer, lambda i, j: (0, j)),
            _lspec((d, tf), layer, lambda i, j: (0, j)),
            _lspec((tf, d), layer, lambda i, j: (j, 0)),
            _lspec((1, d), g_layer, lambda i, j: (0, 0)),
        ],
        out_specs=out_specs,
        out_shape=out_shape,
        compiler_params=pltpu.CompilerParams(
            dimension_semantics=("arbitrary", "arbitrary"), vmem_limit_bytes=_vmem_limit(est)),
        name="ffn",
    )(x, h, w_gate, w_up, w_down, next_g)


def _row_tile(m, target):
    t = min(m, target)
    while m % t:
        t //= 2
    return t


def kernel(x_prompt, x_sample, state_ssm, state_conv, norm1_g, w_in, conv_w, conv_b, dt_bias, a_log, d_skip, ssm_norm_g, sgu_ln_g, sgu_ln_b, sgu_w, sgu_b, w_out_a, w_out_b, w_o, norm2_g, w_ffn_gate, w_ffn_up, w_ffn_down, final_norm_g):
    batch, seq, d = x_prompt.shape
    nseq, steps, _ = x_sample.shape
    depth = w_in.shape[0]
    heads = dt_bias.shape[1]
    d_inner = heads * SSM_HEAD_DIM
    conv_dim = conv_w.shape[2]
    sgu_width = sgu_ln_g.shape[1]
    ngrp = sgu_w.shape[1]
    seg_widths = (2 * sgu_width, d_inner, conv_dim, 2 * d)
    dt_col = 2 * sgu_width + d_inner + conv_dim

    xp = x_prompt.reshape(batch * seq, d)
    xs = x_sample.reshape(nseq * steps, d)
    state_ssm_flat = state_ssm.reshape(depth, nseq, d_inner, SSM_STATE)

    def rows3(p):
        return p.reshape(depth, 1, p.shape[-1])

    w_in_t = jnp.swapaxes(w_in, 1, 2)
    dt_b = rows3(jnp.pad(dt_bias, ((0, 0), (0, LANES - heads))))
    w_dt_exp_t = jnp.repeat(w_in_t[:, dt_col:dt_col + heads, :], SSM_HEAD_DIM, axis=1).astype(BF16)
    dt_b_exp = rows3(jnp.repeat(dt_bias, SSM_HEAD_DIM, axis=1))
    a_log_pad = rows3(jnp.pad(a_log, ((0, 0), (0, LANES - heads))))
    a_exp = rows3(jnp.repeat(-jnp.exp(a_log), SSM_HEAD_DIM, axis=1))
    d_exp = rows3(jnp.repeat(d_skip, SSM_HEAD_DIM, axis=1))
    n1, n2, ng = rows3(norm1_g), rows3(norm2_g), rows3(ssm_norm_g)
    cb, lng, lnb = rows3(conv_b), rows3(sgu_ln_g), rows3(sgu_ln_b)
    tril = jnp.tril(jnp.ones((SGU_CHUNK, SGU_CHUNK), bool))
    mix_p = jnp.where(tril, sgu_w, 0).astype(BF16)
    bias_p = jnp.repeat(jnp.swapaxes(sgu_b, 1, 2), SGU_GROUP_DIM, axis=2)
    seqs_per_chunk = SGU_CHUNK // steps
    eye = jnp.eye(seqs_per_chunk, dtype=F32)
    w_small = jnp.where(jnp.tril(jnp.ones((steps, steps), bool)), sgu_w[:, :, :steps, :steps], 0)
    mix_s = jnp.einsum("ab,lgts->lgatbs", eye, w_small).reshape(depth, ngrp, SGU_CHUNK, SGU_CHUNK).astype(BF16)
    bias_s = jnp.tile(bias_p[:, :steps], (1, seqs_per_chunk, 1))
    wa, wb, wo = w_out_a.astype(BF16), w_out_b.astype(BF16), w_o.astype(BF16)
    wg, wu, wd = w_ffn_gate.astype(BF16), w_ffn_up.astype(BF16), w_ffn_down.astype(BF16)
    expand = (jnp.arange(d_inner)[None, :] // SSM_HEAD_DIM == jnp.arange(LANES)[:, None]).astype(BF16)
    e2 = jnp.concatenate([expand, expand], axis=0)
    rq = jnp.arange((CONV_WIDTH - 1) * SSD_CHUNK)
    shift = (jnp.arange(2 * SSD_CHUNK)[None, :]
             == (CONV_HALO - (CONV_WIDTH - 1) + rq % SSD_CHUNK + rq // SSD_CHUNK)[:, None]).astype(BF16)

    fin_g3 = final_norm_g.reshape(1, 1, d)
    w_su, w_z, w_xbc, w_gt = seg_widths

    def project(h, l, tm, dtype):
        su = _proj(h, w_in_t, l, 0, w_su, jax.nn.gelu, dtype, tm, 1024)
        z = _proj(h, w_in_t, l, w_su, w_z, _identity, dtype, tm, 1024)
        xbc = _proj(h, w_in_t, l, w_su + w_z, w_xbc, _identity, dtype, tm, 1024)
        gt = _proj(h, w_in_t, l, dt_col + heads, w_gt, jax.nn.sigmoid, dtype, tm, 1024)
        return su, z, xbc, gt

    def finish(x, a, yb, gt, l, tm):
        x, h2 = _merge(a, yb, gt, x, wb, wo, n2, l, _row_tile(x.shape[0], 256))
        if l == depth - 1:
            (y,) = _ffn(x, h2, wg, wu, wd, fin_g3, l, 0, tm, 512, True)
            return y, None
        return _ffn(x, h2, wg, wu, wd, n1, l, l + 1, tm, 512, False)

    tm_p = _row_tile(xp.shape[0], 1024)
    tm_s = _row_tile(xs.shape[0], 512)
    hp = _norm(xp, n1, 0, tm_p)
    hs = _norm(xs, n1, 0, tm_s)
    prev_p = prev_s = prev_v = None
    for l in range(depth):
        su, z, xbc, gt = project(hp, l, tm_p, BF16)
        dt = _proj(hp, w_in_t, l, dt_col, LANES, _softplus, F32, tm_p, LANES, bias=dt_b)
        (a,) = _sgu(su, lng, lnb, mix_p, bias_p, wa, l, _row_tile(xp.shape[0], 512), False)
        yb, *prev_p = _ssd_prompt(xbc, z, dt, conv_w, cb, a_log_pad, d_exp, ng, e2, shift, batch, seq, depth, l,
                                  prev_p)
        xp, hp = finish(xp, a, yb, gt, l, _row_tile(xp.shape[0], 512))

        su, z, xbc, gt = project(hs, l, tm_s, F32)
        dte = _proj(hs, w_dt_exp_t, l, 0, d_inner, _softplus, F32, tm_s, 1024, bias=dt_b_exp)
        a, *prev_v = _sgu(su, lng, lnb, mix_s, bias_s, wa, l, tm_s, True, prev_v)
        yb, *prev_s = _ssd_sample(xbc, z, dte, state_conv, state_ssm_flat, l, conv_w, cb, a_exp, d_exp, ng,
                                  steps, prev_s)
        xs, hs = finish(xs, a, yb, gt, l, tm_s)

    ssm_p, conv_p = prev_p
    ssm_s, conv_s = prev_s
    (v_s,) = prev_v
    return (xp.reshape(batch, seq, d), xs.reshape(nseq, steps, d),
            ssm_p.reshape(depth, batch, heads, SSM_HEAD_DIM, SSM_STATE), conv_p,
            ssm_s.reshape(depth, nseq, heads, SSM_HEAD_DIM, SSM_STATE), conv_s,
            v_s.reshape(depth, nseq, steps, sgu_width))
```

```python
import functools

import jax
import jax.numpy as jnp
from jax import lax
from jax.experimental import pallas as pl
from jax.experimental.pallas import tpu as pltpu

F32 = jnp.float32
BF16 = jnp.bfloat16
EPS = 1e-6
LOG2_E = 1.4426950408889634

LANES = 128
SUBLANES = 8
VMEM_CAP_BYTES = 64 * 1024 * 1024

SGU_CHUNK = 128
SGU_GROUP_DIM = 128
SSM_HEAD_DIM = 64
SSM_STATE = 128
SSM_GROUPS = 8
CONV_WIDTH = 4
CONV_HALO = 16
SSD_CHUNK = 128


def _vmem_limit(estimate_bytes):
    return int(min(estimate_bytes + (8 << 20), VMEM_CAP_BYTES - (6 << 20)))


def _silu(x):
    return x * jax.nn.sigmoid(x)


def _softplus(x):
    return jnp.maximum(x, 0.0) + jnp.log1p(jnp.exp(-jnp.abs(x)))


def _rmsnorm(x, g):
    return x * lax.rsqrt(jnp.mean(x * x, axis=-1, keepdims=True) + EPS) * g


def _dot(a, b):
    return jnp.dot(a, b, preferred_element_type=F32)


def _dot_nt(a, b):
    return lax.dot_general(a, b, (((1,), (1,)), ((), ())), preferred_element_type=F32)


def _dot_tn(a, b):
    return lax.dot_general(a, b, (((0,), (0,)), ((), ())), preferred_element_type=F32)


def _chain(args, in_specs, prev, first_out):
    aliases = {}
    if prev is not None:
        for k, arr in enumerate(prev):
            aliases[len(args)] = first_out + k
            args.append(arr)
            in_specs.append(pl.BlockSpec(memory_space=pl.ANY))
    return aliases


def _row_blocks(rows, sub):
    sub = min(rows, sub)
    return [slice(r * sub, (r + 1) * sub) for r in range(rows // sub)]


def _lspec(block, layer, tail_index):
    return pl.BlockSpec((None,) + tuple(block), lambda *ids: (layer,) + tuple(tail_index(*ids)))


def _norm_body(x_ref, g_ref, o_ref):
    o_ref[...] = _rmsnorm(x_ref[...], g_ref[...]).astype(o_ref.dtype)


def _norm(x, g, layer, rows):
    m, d = x.shape
    est = 2 * rows * d * 6 + rows * d * 8
    return pl.pallas_call(
        _norm_body,
        grid=(m // rows,),
        in_specs=[pl.BlockSpec((rows, d), lambda i: (i, 0)), _lspec((1, d), layer, lambda i: (0, 0))],
        out_specs=pl.BlockSpec((rows, d), lambda i: (i, 0)),
        out_shape=jax.ShapeDtypeStruct((m, d), BF16),
        compiler_params=pltpu.CompilerParams(
            dimension_semantics=("arbitrary",), vmem_limit_bytes=_vmem_limit(est)),
        name="norm",
    )(x, g)


def _proj_body(*refs, act, sub, cast, biased, extra, n_main):
    refs = list(refs)
    h_ref = refs.pop(0)
    hx_ref = refs.pop(0) if extra else None
    w_ref = refs.pop(0)
    b_ref = refs.pop(0) if biased else None
    o_ref = refs.pop(0)
    ox_ref = refs.pop(0) if extra else None
    i = pl.program_id(1)
    if cast:
        (w_bf,) = refs

        @pl.when(i == 0)
        def _():
            w_bf[...] = w_ref[0].astype(BF16)
    else:
        w_bf = w_ref.at[0]

    def run(src_ref, dst_ref):
        def body():
            for rs in _row_blocks(src_ref.shape[0], sub):
                t = _dot_nt(src_ref[rs, :], w_bf[...])
                if biased:
                    t = t + b_ref[...]
                dst_ref[rs, :] = act(t).astype(dst_ref.dtype)
        return body

    if extra:
        pl.when(i < n_main)(run(h_ref, o_ref))
        pl.when(i == n_main)(run(hx_ref, ox_ref))
    else:
        run(h_ref, o_ref)()


def _proj(h, w_t, layer, col0, width, act, out_dtype, tm, tn, bias=None, extra=None):
    m, d = h.shape
    cast = w_t.dtype != BF16
    n_main = m // tm
    osz = jnp.dtype(out_dtype).itemsize
    est = 2 * (tm * d * 2 + d * tn * w_t.dtype.itemsize + tm * tn * osz + tn * 4) + d * tn * 2 + 256 * tn * 8
    last = n_main - 1
    in_specs = [pl.BlockSpec((tm, d), lambda j, i: (jnp.minimum(i, last), 0))]
    args = [h]
    out_specs = [pl.BlockSpec((tm, tn), lambda j, i: (jnp.minimum(i, last), j))]
    out_shape = [jax.ShapeDtypeStruct((m, width), out_dtype)]
    if extra is not None:
        hx, xdtype = extra
        mx = hx.shape[0]
        in_specs.append(pl.BlockSpec((mx, d), lambda j, i: (0, 0)))
        args.append(hx)
        out_specs.append(pl.BlockSpec((mx, tn), lambda j, i: (0, j)))
        out_shape.append(jax.ShapeDtypeStruct((mx, width), xdtype))
        est += 2 * (mx * d * 2 + mx * tn * jnp.dtype(xdtype).itemsize)
    in_specs.append(pl.BlockSpec((pl.Element(1), pl.Element(tn), pl.Element(d)),
                                 lambda j, i: (layer, pl.multiple_of(col0 + j * tn, SUBLANES), 0)))
    args.append(w_t)
    if bias is not None:
        in_specs.append(_lspec((1, tn), layer, lambda j, i: (0, j)))
        args.append(bias)
    outs = pl.pallas_call(
        functools.partial(_proj_body, act=act, sub=256, cast=cast, biased=bias is not None,
                          extra=extra is not None, n_main=n_main),
        grid=(width // tn, n_main + (extra is not None)),
        in_specs=in_specs,
        out_specs=out_specs,
        out_shape=out_shape,
        scratch_shapes=[pltpu.VMEM((tn, d), BF16)] if cast else [],
        compiler_params=pltpu.CompilerParams(
            dimension_semantics=("arbitrary", "arbitrary"), vmem_limit_bytes=_vmem_limit(est)),
        name="proj",
    )(*args)
    return outs if extra is not None else outs[0]


def _identity(t):
    return t


def _sgu_body(u_ref, v_ref, lng_ref, lnb_ref, mix_ref, bias_ref, wa_ref, *rest, emit_v, chained):
    if emit_v:
        if chained:
            rest = rest[1:]
        a_ref, vout_ref, ya_scr = rest
    else:
        a_ref, ya_scr = rest
    vg = v_ref[...].astype(F32)
    mu = jnp.mean(vg, axis=-1, keepdims=True)
    var = jnp.mean(jnp.square(vg - mu), axis=-1, keepdims=True)
    v = (vg - mu) * lax.rsqrt(var + EPS) * lng_ref[...] + lnb_ref[...]
    if emit_v:
        vout_ref[...] = v
    vb = v.astype(BF16)
    rows, width = vb.shape
    for c in range(rows // SGU_CHUNK):
        rs = slice(c * SGU_CHUNK, (c + 1) * SGU_CHUNK)
        for grp in range(width // SGU_GROUP_DIM):
            cs = slice(grp * SGU_GROUP_DIM, (grp + 1) * SGU_GROUP_DIM)
            mixed = _dot(mix_ref[grp], vb[rs, cs]) + bias_ref[:, cs]
            ya_scr[rs, cs] = (u_ref[rs, cs].astype(F32) * mixed).astype(BF16)
    a_ref[...] = _dot(ya_scr[...], wa_ref[...]).astype(a_ref.dtype)


def _sgu(su, lng, lnb, mix, bias, w_out_a, layer, rows, emit_v, prev=None):
    m = su.shape[0]
    width = su.shape[1] // 2
    depth, _, d = w_out_a.shape
    ngrp = mix.shape[1]
    isz = su.dtype.itemsize
    est = 2 * (2 * rows * width * isz + ngrp * SGU_CHUNK * SGU_CHUNK * 2 + SGU_CHUNK * width * 4
               + rows * d * 2 + rows * width * 4) + width * d * 2 + 4 * rows * width * 4
    out_shape = [jax.ShapeDtypeStruct((m, d), BF16)]
    out_specs = [pl.BlockSpec((rows, d), lambda i: (i, 0))]
    if emit_v:
        out_shape.append(jax.ShapeDtypeStruct((depth, m, width), F32))
        out_specs.append(pl.BlockSpec((None, rows, width), lambda i: (layer, i, 0)))
    in_specs = [
        pl.BlockSpec((rows, width), lambda i: (i, 0)),
        pl.BlockSpec((rows, width), lambda i: (i, 1)),
        _lspec((1, width), layer, lambda i: (0, 0)),
        _lspec((1, width), layer, lambda i: (0, 0)),
        _lspec((ngrp, SGU_CHUNK, SGU_CHUNK), layer, lambda i: (0, 0, 0)),
        _lspec((SGU_CHUNK, width), layer, lambda i: (0, 0)),
        pl.BlockSpec((None, width, d), lambda i: (layer, 0, 0), pipeline_mode=pl.Buffered(1)),
    ]
    args = [su, su, lng, lnb, mix, bias, w_out_a]
    aliases = _chain(args, in_specs, prev, 1)
    return pl.pallas_call(
        functools.partial(_sgu_body, emit_v=emit_v, chained=prev is not None),
        grid=(m // rows,),
        in_specs=in_specs,
        out_specs=out_specs,
        out_shape=out_shape,
        scratch_shapes=[pltpu.VMEM((rows, width), BF16)],
        input_output_aliases=aliases,
        compiler_params=pltpu.CompilerParams(
            dimension_semantics=("arbitrary",), vmem_limit_bytes=_vmem_limit(est)),
        name="sgu",
    )(*args)


def _prefix_sum_rows(x, row):
    n = x.shape[0]
    s = 1
    while s < n:
        x = x + jnp.where(row >= s, pltpu.roll(x, s, 0), 0.0)
        s *= 2
    return x


def _gate_groupnorm(y, z, ng):
    t = y * _silu(z)
    return t * lax.rsqrt(jnp.mean(t * t, axis=-1, keepdims=True) + EPS) * ng


def _split_hi_lo(v):
    hi = v.astype(BF16)
    lo = (v - hi.astype(F32)).astype(BF16)
    return jnp.concatenate([hi, lo], axis=1)


def _ssd_prompt_body(xbc_ref, z_ref, dt_ref, cw_ref, cb_ref, alog_ref, dexp_ref, ng_ref, e2_ref, shift_ref, *rest,
                     d_inner, n_chunks, chained):
    if chained:
        rest = rest[2:]
    yb_ref, h_ref, cst_ref, win_scr, act_scr, y_scr, ht_scr, cst_scr = rest
    c = pl.program_id(1)
    q = xbc_ref.shape[0]
    conv_dim = xbc_ref.shape[1]
    p = SSM_HEAD_DIM
    n_state = SSM_STATE
    gw = d_inner // SSM_GROUPS
    pairs_per_group = gw // (2 * p)
    halo = CONV_HALO

    @pl.when(c == 0)
    def _():
        win_scr[...] = jnp.zeros(win_scr.shape, BF16)
        ht_scr[...] = jnp.zeros(ht_scr.shape, F32)

    win_scr[halo:halo + q, :] = xbc_ref[...]
    lane_chunk = 512
    for j in range(conv_dim // lane_chunk):
        ls = slice(j * lane_chunk, (j + 1) * lane_chunk)
        shifted = _dot(shift_ref[...], win_scr[:, ls])
        acc = cb_ref[:, ls] + cw_ref[CONV_WIDTH - 1:CONV_WIDTH, ls] * xbc_ref[:, ls].astype(F32)
        for k in range(CONV_WIDTH - 1):
            acc = acc + cw_ref[k:k + 1, ls] * shifted[k * q:(k + 1) * q, :]
        act_scr[:, ls] = _silu(acc)

    @pl.when(c == n_chunks - 1)
    def _():
        tail = xbc_ref[q - halo:q, :].astype(F32)
        cst_ref[...] = tail[halo - (CONV_WIDTH - 1):halo, :]

    win_scr[0:halo, :] = xbc_ref[q - halo:q, :]

    row = lax.broadcasted_iota(jnp.int32, (q, LANES), 0)
    lane = lax.broadcasted_iota(jnp.int32, (q, LANES), 1)
    dt = dt_ref[...]
    da = dt * (-jnp.exp(alog_ref[...]))
    cs = _prefix_sum_rows(da, row)
    cs2 = cs * LOG2_E
    cst_scr[...] = cs2.T
    dt_hl = _split_hi_lo(dt)
    ecs_hl = _split_hi_lo(jnp.exp(cs))
    dec_hl = _split_hi_lo(jnp.exp(cs[q - 1:q, :] - cs))
    causal = row >= lane
    lower_half = lane < p

    for grp in range(SSM_GROUPS):
        gs = slice(grp * gw, (grp + 1) * gw)
        e2 = e2_ref[:, gs]
        dt_g = _dot(dt_hl, e2)
        ecs_g = _dot(ecs_hl, e2)
        dec_g = _dot(dec_hl, e2)
        xs_g = act_scr[:, gs]
        xdt_g = xs_g * dt_g
        b_g = act_scr[:, d_inner + grp * n_state:d_inner + (grp + 1) * n_state].astype(BF16)
        c_g = act_scr[:, d_inner + (SSM_GROUPS + grp) * n_state:
                      d_inner + (SSM_GROUPS + grp + 1) * n_state].astype(BF16)
        cbm = jnp.where(causal, _dot_nt(c_g, b_g), 0.0)
        ht_g = ht_scr[:, gs]
        y_g = _dot(c_g, ht_g.astype(BF16)) * ecs_g + dexp_ref[:, gs] * xs_g
        for j in range(pairs_per_group):
            h0 = grp * 2 * pairs_per_group + 2 * j
            ws = []
            for h in (h0, h0 + 1):
                seg2 = cs2[:, h:h + 1] - cst_scr[h:h + 1, :]
                ws.append((cbm * jnp.exp2(jnp.where(causal, seg2, 0.0))).astype(BF16))
            xpair = xdt_g[:, j * LANES:(j + 1) * LANES]
            rhs = jnp.concatenate([jnp.where(lower_half, xpair, 0.0).astype(BF16),
                                   jnp.where(lower_half, 0.0, xpair).astype(BF16)], axis=0)
            ls = slice(grp * gw + j * LANES, grp * gw + (j + 1) * LANES)
            y_scr[:, ls] = y_g[:, j * LANES:(j + 1) * LANES] + _dot(jnp.concatenate(ws, axis=1), rhs)
        states_t = _dot_tn(b_g, (xdt_g * dec_g).astype(BF16))
        ht_scr[:, gs] = ht_g * ecs_g[q - 1:q, :] + states_t

    for grp in range(SSM_GROUPS):
        gs = slice(grp * gw, (grp + 1) * gw)
        yb_ref[:, gs] = _gate_groupnorm(y_scr[:, gs], z_ref[:, gs].astype(F32), ng_ref[:, gs]).astype(yb_ref.dtype)

    @pl.when(c == n_chunks - 1)
    def _():
        for grp in range(SSM_GROUPS):
            gs = slice(grp * gw, (grp + 1) * gw)
            h_ref[gs, :] = ht_scr[:, gs].T


def _ssd_prompt(xbc, z, dt, conv_w, conv_b, a_log, d_exp, norm_g, e2, shift, batch, seq, depth, layer, prev):
    q = SSD_CHUNK
    n_chunks = seq // q
    conv_dim = xbc.shape[1]
    d_inner = z.shape[1]
    dtw = dt.shape[1]
    est = (2 * (q * conv_dim * 2 + q * d_inner * 2 + q * dtw * 4 + q * d_inner * 2 + d_inner * SSM_STATE * 4
                + 8 * conv_dim * 4 + 5 * conv_dim * 4 + 2 * d_inner * 4 + 2 * LANES * d_inner * 2)
           + (q + 16) * conv_dim * 4 + q * conv_dim * 4 + 2 * q * d_inner * 4 + (8 << 20))
    row_map = lambda b, c: (b * n_chunks + c, 0)
    const = lambda b, c: (0, 0)
    in_specs = [
        pl.BlockSpec((q, conv_dim), row_map),
        pl.BlockSpec((q, d_inner), row_map),
        pl.BlockSpec((q, dtw), row_map),
        _lspec((CONV_WIDTH, conv_dim), layer, const),
        _lspec((1, conv_dim), layer, const),
        _lspec((1, dtw), layer, const),
        _lspec((1, d_inner), layer, const),
        _lspec((1, d_inner), layer, const),
        pl.BlockSpec((2 * LANES, d_inner), const),
        pl.BlockSpec(shift.shape, const),
    ]
    args = [xbc, z, dt, conv_w, conv_b, a_log, d_exp, norm_g, e2, shift]
    aliases = _chain(args, in_specs, prev, 1)
    return pl.pallas_call(
        functools.partial(_ssd_prompt_body, d_inner=d_inner, n_chunks=n_chunks, chained=prev is not None),
        grid=(batch, n_chunks),
        in_specs=in_specs,
        out_specs=[
            pl.BlockSpec((q, d_inner), row_map),
            pl.BlockSpec((None, None, d_inner, SSM_STATE), lambda b, c: (layer, b, 0, 0)),
            pl.BlockSpec((None, None, CONV_WIDTH - 1, conv_dim), lambda b, c: (layer, b, 0, 0)),
        ],
        out_shape=[
            jax.ShapeDtypeStruct((batch * seq, d_inner), BF16),
            jax.ShapeDtypeStruct((depth, batch, d_inner, SSM_STATE), F32),
            jax.ShapeDtypeStruct((depth, batch, CONV_WIDTH - 1, conv_dim), F32),
        ],
        scratch_shapes=[
            pltpu.VMEM((2 * q, conv_dim), BF16),
            pltpu.VMEM((q, conv_dim), F32),
            pltpu.VMEM((q, d_inner), F32),
            pltpu.VMEM((SSM_STATE, d_inner), F32),
            pltpu.VMEM((LANES, q), F32),
        ],
        input_output_aliases=aliases,
        compiler_params=pltpu.CompilerParams(
            dimension_semantics=("arbitrary", "arbitrary"), vmem_limit_bytes=_vmem_limit(est)),
        name="ssd_prompt",
    )(*args)


def _ssd_sample_body(xbc_ref, z_ref, dte_ref, cbuf_ref, h0_ref, cw_ref, cb_ref, aexp_ref, dexp_ref, ng_ref,
                     *rest, d_inner, steps, chained):
    if chained:
        rest = rest[2:]
    yb_ref, h_ref, cst_ref, win_scr = rest
    q = xbc_ref.shape[0]
    nseq = q // steps
    conv_dim = xbc_ref.shape[1]
    n_state = SSM_STATE
    gw = d_inner // SSM_GROUPS
    halo = SUBLANES
    span = 2 * SUBLANES

    win_scr[...] = jnp.zeros(win_scr.shape, F32)
    for j in range(nseq):
        win_scr[j * span + halo - 3:j * span + halo, :] = cbuf_ref[j]
        win_scr[j * span + halo:j * span + halo + steps, :] = xbc_ref[j * steps:(j + 1) * steps, :]
        cst_ref[j] = win_scr[j * span + halo + steps - 3:j * span + halo + steps, :]
    rowc = lax.broadcasted_iota(jnp.int32, (q, conv_dim), 0)
    acc = None
    for j in range(nseq):
        base = j * span + halo - 3 - j * steps
        acc_j = cb_ref[...] + cw_ref[0:1, :] * win_scr[base:base + q, :]
        for k in range(1, CONV_WIDTH):
            acc_j = acc_j + cw_ref[k:k + 1, :] * win_scr[base + k:base + k + q, :]
        acc = acc_j if acc is None else jnp.where(rowc >= j * steps, acc_j, acc)
    act = _silu(acc)

    xs = act[:, :d_inner]
    row = lax.broadcasted_iota(jnp.int32, (q, d_inner), 0)
    tok = row % steps
    dt = dte_ref[...]
    da = dt * aexp_ref[...]
    cs = da
    s = 1
    while s < steps:
        cs = cs + jnp.where(tok >= s, pltpu.roll(cs, s, 0), 0.0)
        s *= 2
    cs_last = cs[steps - 1:steps, :]
    for j in range(1, nseq):
        cs_last = jnp.where(row >= j * steps, cs[(j + 1) * steps - 1:(j + 1) * steps, :], cs_last)
    xdt = xs * dt

    b_all = act[:, d_inner:d_inner + SSM_GROUPS * n_state]
    c_all = act[:, d_inner + SSM_GROUPS * n_state:]
    y = dexp_ref[...] * xs
    for o in range(steps):
        b_sh = b_all if o == 0 else pltpu.roll(b_all, o, 0)
        cs_sh = cs if o == 0 else pltpu.roll(cs, o, 0)
        xdt_sh = xdt if o == 0 else pltpu.roll(xdt, o, 0)
        prod = c_all * b_sh
        cb_o = jnp.concatenate(
            [jnp.broadcast_to(jnp.sum(prod[:, grp * n_state:(grp + 1) * n_state], axis=1, keepdims=True), (q, gw))
             for grp in range(SSM_GROUPS)], axis=1)
        keep = tok >= o
        lmat = jnp.where(keep, jnp.exp(jnp.where(keep, cs - cs_sh, 0.0)), 0.0)
        y = y + cb_o * lmat * xdt_sh

    ecs = jnp.exp(cs)
    xs_dec = xdt * jnp.exp(cs_last - cs)
    cd = jnp.exp(cs_last)
    rowg = lax.broadcasted_iota(jnp.int32, (q, gw), 0)
    lane = lax.broadcasted_iota(jnp.int32, (q, LANES), 1)
    p = SSM_HEAD_DIM
    for grp in range(SSM_GROUPS):
        gs = slice(grp * gw, (grp + 1) * gw)
        b_g = b_all[:, grp * n_state:(grp + 1) * n_state].astype(BF16)
        c_g = c_all[:, grp * n_state:(grp + 1) * n_state].astype(BF16)
        head_cd = []
        for t in range(gw // LANES):
            v = cd[:, grp * gw + t * LANES:grp * gw + (t + 1) * LANES]
            vr = pltpu.roll(v, p, 1)
            head_cd += [jnp.where(lane < p, v, vr), jnp.where(lane < p, vr, v)]
        y_off = None
        for j in range(nseq):
            h_g = h0_ref[j, gs, :]
            mine = (rowg >= j * steps) & (rowg < (j + 1) * steps)
            y_off_j = _dot_nt(c_g, h_g.astype(BF16))
            y_off = y_off_j if y_off is None else jnp.where(mine, y_off_j, y_off)
            states = _dot_tn(jnp.where(mine, xs_dec[:, gs], 0.0).astype(BF16), b_g)
            scale = jnp.concatenate(
                [jnp.broadcast_to(hc[j * steps:j * steps + 1, :], (p, n_state)) for hc in head_cd], axis=0)
            h_ref[j, gs, :] = h_g * scale + states
        y_g = y[:, gs] + y_off * ecs[:, gs]
        yb_ref[:, gs] = _gate_groupnorm(y_g, z_ref[:, gs].astype(F32), ng_ref[:, gs]).astype(yb_ref.dtype)


def _ssd_sample(xbc, z, dte, conv_state, h0_all, layer, conv_w, conv_b, a_exp, d_exp, norm_g, steps, prev):
    m, conv_dim = xbc.shape
    d_inner = z.shape[1]
    depth, nseq_total = h0_all.shape[:2]
    q = SUBLANES
    nseq = q // steps
    est = (2 * (2 * nseq * d_inner * SSM_STATE * 4 + q * (conv_dim + 3 * d_inner) * 4
                + 2 * nseq * SUBLANES * conv_dim * 4 + 5 * conv_dim * 4 + 3 * d_inner * 4)
           + 2 * nseq * SUBLANES * conv_dim * 4 + d_inner * LANES * 4 + (12 << 20))
    row_map = lambda s: (s, 0)
    lay_map = lambda s: (layer, s, 0, 0)
    const = lambda s: (0, 0)
    in_specs = [
        pl.BlockSpec((q, conv_dim), row_map),
        pl.BlockSpec((q, d_inner), row_map),
        pl.BlockSpec((q, d_inner), row_map),
        pl.BlockSpec((None, nseq, CONV_WIDTH - 1, conv_dim), lay_map),
        pl.BlockSpec((None, nseq, d_inner, SSM_STATE), lay_map),
        _lspec((CONV_WIDTH, conv_dim), layer, const),
        _lspec((1, conv_dim), layer, const),
        _lspec((1, d_inner), layer, const),
        _lspec((1, d_inner), layer, const),
        _lspec((1, d_inner), layer, const),
    ]
    args = [xbc, z, dte, conv_state, h0_all, conv_w, conv_b, a_exp, d_exp, norm_g]
    aliases = _chain(args, in_specs, prev, 1)
    return pl.pallas_call(
        functools.partial(_ssd_sample_body, d_inner=d_inner, steps=steps, chained=prev is not None),
        grid=(m // q,),
        in_specs=in_specs,
        out_specs=[
            pl.BlockSpec((q, d_inner), row_map),
            pl.BlockSpec((None, nseq, d_inner, SSM_STATE), lay_map),
            pl.BlockSpec((None, nseq, CONV_WIDTH - 1, conv_dim), lay_map),
        ],
        out_shape=[
            jax.ShapeDtypeStruct((m, d_inner), F32),
            jax.ShapeDtypeStruct((depth, nseq_total, d_inner, SSM_STATE), F32),
            jax.ShapeDtypeStruct((depth, nseq_total, CONV_WIDTH - 1, conv_dim), F32),
        ],
        scratch_shapes=[pltpu.VMEM((nseq * 2 * SUBLANES, conv_dim), F32)],
        input_output_aliases=aliases,
        compiler_params=pltpu.CompilerParams(
            dimension_semantics=("arbitrary",), vmem_limit_bytes=_vmem_limit(est)),
        name="ssd_sample",
    )(*args)


def _merge_body(a_ref, yb_ref, ga_ref, gb_ref, x_ref, wb_ref, wo_ref, g_ref, o_ref, h_ref):
    b = _dot(yb_ref[...].astype(BF16), wb_ref[...])
    merged = ga_ref[...].astype(F32) * a_ref[...].astype(F32) + gb_ref[...].astype(F32) * b
    x1 = x_ref[...] + _dot(merged.astype(BF16), wo_ref[...])
    o_ref[...] = x1
    h_ref[...] = _rmsnorm(x1, g_ref[...]).astype(h_ref.dtype)


def _merge(a, yb, gates, x, w_out_b, w_o, norm_g, layer, rows):
    m, d = x.shape
    d_inner = yb.shape[1]
    est = (2 * rows * (d * 2 + d_inner * yb.dtype.itemsize + 2 * d * gates.dtype.itemsize + 2 * d * 4 + d * 2)
           + d_inner * d * 2 + d * d * 2 + 3 * rows * d * 4)
    return pl.pallas_call(
        _merge_body,
        grid=(m // rows,),
        in_specs=[
            pl.BlockSpec((rows, d), lambda i: (i, 0)),
            pl.BlockSpec((rows, d_inner), lambda i: (i, 0)),
            pl.BlockSpec((rows, d), lambda i: (i, 0)),
            pl.BlockSpec((rows, d), lambda i: (i, 1)),
            pl.BlockSpec((rows, d), lambda i: (i, 0)),
            pl.BlockSpec((None, d_inner, d), lambda i: (layer, 0, 0), pipeline_mode=pl.Buffered(1)),
            pl.BlockSpec((None, d, d), lambda i: (layer, 0, 0), pipeline_mode=pl.Buffered(1)),
            _lspec((1, d), layer, lambda i: (0, 0)),
        ],
        out_specs=[pl.BlockSpec((rows, d), lambda i: (i, 0)), pl.BlockSpec((rows, d), lambda i: (i, 0))],
        out_shape=[jax.ShapeDtypeStruct((m, d), F32), jax.ShapeDtypeStruct((m, d), BF16)],
        compiler_params=pltpu.CompilerParams(
            dimension_semantics=("arbitrary",), vmem_limit_bytes=_vmem_limit(est)),
        name="merge",
    )(a, yb, gates, gates, x, w_out_b, w_o, norm_g)


def _ffn_body(x_ref, h_ref, wg_ref, wu_ref, wd_ref, g_ref, o_ref, *rest, n_f, final, sub):
    f = pl.program_id(1)
    blocks = _row_blocks(x_ref.shape[0], sub)

    @pl.when(f == 0)
    def _():
        for rs in blocks:
            o_ref[rs, :] = x_ref[rs, :]

    for rs in blocks:
        hb = h_ref[rs, :]
        act = _silu(_dot(hb, wg_ref[...])) * _dot(hb, wu_ref[...])
        o_ref[rs, :] += _dot(act.astype(BF16), wd_ref[...])

    @pl.when(f == n_f - 1)
    def _():
        for rs in blocks:
            y = _rmsnorm(o_ref[rs, :], g_ref[...])
            if final:
                o_ref[rs, :] = y
            else:
                rest[0][rs, :] = y.astype(BF16)


def _ffn(x, h, w_gate, w_up, w_down, next_g, layer, g_layer, tm, tf, final):
    m, d = x.shape
    hidden = w_gate.shape[2]
    n_f = hidden // tf
    est = 2 * (tm * d * 4 + tm * d * 2 + tm * d * 4 + tm * d * 2 + 3 * d * tf * 2) + 256 * (2 * tf + d) * 4
    out_specs = [pl.BlockSpec((tm, d), lambda i, j: (i, 0))]
    out_shape = [jax.ShapeDtypeStruct((m, d), F32)]
    if not final:
        out_specs.append(pl.BlockSpec((tm, d), lambda i, j: (i, 0)))
        out_shape.append(jax.ShapeDtypeStruct((m, d), BF16))
    return pl.pallas_call(
        functools.partial(_ffn_body, n_f=n_f, final=final, sub=256),
        grid=(m // tm, n_f),
        in_specs=[
            pl.BlockSpec((tm, d), lambda i, j: (i, 0)),
            pl.BlockSpec((tm, d), lambda i, j: (i, 0)),
            _lspec((d, tf), layer, lambda i, j: (0, j)),
            _lspec((d, tf), layer, lambda i, j: (0, j)),
            _lspec((tf, d), layer, lambda i, j: (j, 0)),
            _lspec((1, d), g_layer, lambda i, j: (0, 0)),
        ],
        out_specs=out_specs,
        out_shape=out_shape,
        compiler_params=pltpu.CompilerParams(
            dimension_semantics=("arbitrary", "arbitrary"), vmem_limit_bytes=_vmem_limit(est)),
        name="ffn",
    )(x, h, w_gate, w_up, w_down, next_g)


def _row_tile(m, target):
    t = min(m, target)
    while m % t:
        t //= 2
    return t


def kernel(x_prompt, x_sample, state_ssm, state_conv, norm1_g, w_in, conv_w, conv_b, dt_bias, a_log, d_skip, ssm_norm_g, sgu_ln_g, sgu_ln_b, sgu_w, sgu_b, w_out_a, w_out_b, w_o, norm2_g, w_ffn_gate, w_ffn_up, w_ffn_down, final_norm_g):
    batch, seq, d = x_prompt.shape
    nseq, steps, _ = x_sample.shape
    depth = w_in.shape[0]
    heads = dt_bias.shape[1]
    d_inner = heads * SSM_HEAD_DIM
    conv_dim = conv_w.shape[2]
    sgu_width = sgu_ln_g.shape[1]
    ngrp = sgu_w.shape[1]
    seg_widths = (2 * sgu_width, d_inner, conv_dim, 2 * d)
    dt_col = 2 * sgu_width + d_inner + conv_dim

    xp = x_prompt.reshape(batch * seq, d)
    xs = x_sample.reshape(nseq * steps, d)
    state_ssm_flat = state_ssm.reshape(depth, nseq, d_inner, SSM_STATE)

    def rows3(p):
        return p.reshape(depth, 1, p.shape[-1])

    w_in_t = jnp.swapaxes(w_in, 1, 2)
    dt_b = rows3(jnp.pad(dt_bias, ((0, 0), (0, LANES - heads))))
    w_dt_exp_t = jnp.repeat(w_in_t[:, dt_col:dt_col + heads, :], SSM_HEAD_DIM, axis=1).astype(BF16)
    dt_b_exp = rows3(jnp.repeat(dt_bias, SSM_HEAD_DIM, axis=1))
    a_log_pad = rows3(jnp.pad(a_log, ((0, 0), (0, LANES - heads))))
    a_exp = rows3(jnp.repeat(-jnp.exp(a_log), SSM_HEAD_DIM, axis=1))
    d_exp = rows3(jnp.repeat(d_skip, SSM_HEAD_DIM, axis=1))
    n1, n2, ng = rows3(norm1_g), rows3(norm2_g), rows3(ssm_norm_g)
    cb, lng, lnb = rows3(conv_b), rows3(sgu_ln_g), rows3(sgu_ln_b)
    tril = jnp.tril(jnp.ones((SGU_CHUNK, SGU_CHUNK), bool))
    mix_p = jnp.where(tril, sgu_w, 0).astype(BF16)
    bias_p = jnp.repeat(jnp.swapaxes(sgu_b, 1, 2), SGU_GROUP_DIM, axis=2)
    seqs_per_chunk = SGU_CHUNK // steps
    eye = jnp.eye(seqs_per_chunk, dtype=F32)
    w_small = jnp.where(jnp.tril(jnp.ones((steps, steps), bool)), sgu_w[:, :, :steps, :steps], 0)
    mix_s = jnp.einsum("ab,lgts->lgatbs", eye, w_small).reshape(depth, ngrp, SGU_CHUNK, SGU_CHUNK).astype(BF16)
    bias_s = jnp.tile(bias_p[:, :steps], (1, seqs_per_chunk, 1))
    wa, wb, wo = w_out_a.astype(BF16), w_out_b.astype(BF16), w_o.astype(BF16)
    wg, wu, wd = w_ffn_gate.astype(BF16), w_ffn_up.astype(BF16), w_ffn_down.astype(BF16)
    expand = (jnp.arange(d_inner)[None, :] // SSM_HEAD_DIM == jnp.arange(LANES)[:, None]).astype(BF16)
    e2 = jnp.concatenate([expand, expand], axis=0)
    rq = jnp.arange((CONV_WIDTH - 1) * SSD_CHUNK)
    shift = (jnp.arange(2 * SSD_CHUNK)[None, :]
             == (CONV_HALO - (CONV_WIDTH - 1) + rq % SSD_CHUNK + rq // SSD_CHUNK)[:, None]).astype(BF16)

    fin_g3 = final_norm_g.reshape(1, 1, d)
    w_su, w_z, w_xbc, w_gt = seg_widths

    def project(h, h_sample, l, tm):
        segs = ((0, w_su, jax.nn.gelu), (w_su, w_z, _identity), (w_su + w_z, w_xbc, _identity),
                (dt_col + heads, w_gt, jax.nn.sigmoid))
        outs = [_proj(h, w_in_t, l, c0, wdt, act, BF16, tm, 1024, extra=(h_sample, F32)) for c0, wdt, act in segs]
        return [o[0] for o in outs], [o[1] for o in outs]

    def finish(x, a, yb, gt, l, tm):
        x, h2 = _merge(a, yb, gt, x, wb, wo, n2, l, _row_tile(x.shape[0], 256))
        if l == depth - 1:
            (y,) = _ffn(x, h2, wg, wu, wd, fin_g3, l, 0, tm, 512, True)
            return y, None
        return _ffn(x, h2, wg, wu, wd, n1, l, l + 1, tm, 512, False)

    tm_p = _row_tile(xp.shape[0], 1024)
    tm_s = _row_tile(xs.shape[0], 512)
    hp = _norm(xp, n1, 0, tm_p)
    hs = _norm(xs, n1, 0, tm_s)
    prev_p = prev_s = prev_v = None
    for l in range(depth):
        (su, z, xbc, gt), sample_proj = project(hp, hs, l, tm_p)

        dt = _proj(hp, w_in_t, l, dt_col, LANES, _softplus, F32, tm_p, LANES, bias=dt_b)
        (a,) = _sgu(su, lng, lnb, mix_p, bias_p, wa, l, _row_tile(xp.shape[0], 512), False)
        yb, *prev_p = _ssd_prompt(xbc, z, dt, conv_w, cb, a_log_pad, d_exp, ng, e2, shift, batch, seq, depth, l,
                                  prev_p)
        xp, hp = finish(xp, a, yb, gt, l, _row_tile(xp.shape[0], 512))

        su, z, xbc, gt = sample_proj
        dte = _proj(hs, w_dt_exp_t, l, 0, d_inner, _softplus, F32, tm_s, 1024, bias=dt_b_exp)
        a, *prev_v = _sgu(su, lng, lnb, mix_s, bias_s, wa, l, tm_s, True, prev_v)
        yb, *prev_s = _ssd_sample(xbc, z, dte, state_conv, state_ssm_flat, l, conv_w, cb, a_exp, d_exp, ng,
                                  steps, prev_s)
        xs, hs = finish(xs, a, yb, gt, l, tm_s)

    ssm_p, conv_p = prev_p
    ssm_s, conv_s = prev_s
    (v_s,) = prev_v
    return (xp.reshape(batch, seq, d), xs.reshape(nseq, steps, d),
            ssm_p.reshape(depth, batch, heads, SSM_HEAD_DIM, SSM_STATE), conv_p,
            ssm_s.reshape(depth, nseq, heads, SSM_HEAD_DIM, SSM_STATE), conv_s,
            v_s.reshape(depth, nseq, steps, sgu_width))
```

```python
import functools

import jax
import jax.numpy as jnp
from jax import lax
from jax.experimental import pallas as pl
from jax.experimental.pallas import tpu as pltpu

F32 = jnp.float32
BF16 = jnp.bfloat16
EPS = 1e-6
LOG2_E = 1.4426950408889634

LANES = 128
SUBLANES = 8
VMEM_CAP_BYTES = 64 * 1024 * 1024

SGU_CHUNK = 128
SGU_GROUP_DIM = 128
SSM_HEAD_DIM = 64
SSM_STATE = 128
SSM_GROUPS = 8
CONV_WIDTH = 4
FFN_TILE = 512
CONV_HALO = 16
SSD_CHUNK = 128


def _vmem_limit(estimate_bytes):
    return int(min(estimate_bytes + (8 << 20), VMEM_CAP_BYTES - (6 << 20)))


def _silu(x):
    return x * jax.nn.sigmoid(x)


def _softplus(x):
    return jnp.maximum(x, 0.0) + jnp.log1p(jnp.exp(-jnp.abs(x)))


def _rmsnorm(x, g):
    return x * lax.rsqrt(jnp.mean(x * x, axis=-1, keepdims=True) + EPS) * g


def _dot(a, b):
    return jnp.dot(a, b, preferred_element_type=F32)


def _dot_nt(a, b):
    return lax.dot_general(a, b, (((1,), (1,)), ((), ())), preferred_element_type=F32)


def _dot_tn(a, b):
    return lax.dot_general(a, b, (((0,), (0,)), ((), ())), preferred_element_type=F32)


def _chain(args, in_specs, prev, first_out):
    aliases = {}
    if prev is not None:
        for k, arr in enumerate(prev):
            aliases[len(args)] = first_out + k
            args.append(arr)
            in_specs.append(pl.BlockSpec(memory_space=pl.ANY))
    return aliases


def _row_blocks(rows, sub):
    sub = min(rows, sub)
    return [slice(r * sub, (r + 1) * sub) for r in range(rows // sub)]


def _lspec(block, layer, tail_index):
    return pl.BlockSpec((None,) + tuple(block), lambda *ids: (layer,) + tuple(tail_index(*ids)))


def _norm_body(x_ref, g_ref, o_ref):
    o_ref[...] = _rmsnorm(x_ref[...], g_ref[...]).astype(o_ref.dtype)


def _norm(x, g, layer, rows):
    m, d = x.shape
    est = 2 * rows * d * 6 + rows * d * 8
    return pl.pallas_call(
        _norm_body,
        grid=(m // rows,),
        in_specs=[pl.BlockSpec((rows, d), lambda i: (i, 0)), _lspec((1, d), layer, lambda i: (0, 0))],
        out_specs=pl.BlockSpec((rows, d), lambda i: (i, 0)),
        out_shape=jax.ShapeDtypeStruct((m, d), BF16),
        compiler_params=pltpu.CompilerParams(
            dimension_semantics=("arbitrary",), vmem_limit_bytes=_vmem_limit(est)),
        name="norm",
    )(x, g)


def _proj_body(*refs, act, sub, cast, biased, extra, n_main):
    refs = list(refs)
    h_ref = refs.pop(0)
    hx_ref = refs.pop(0) if extra else None
    w_ref = refs.pop(0)
    b_ref = refs.pop(0) if biased else None
    o_ref = refs.pop(0)
    ox_ref = refs.pop(0) if extra else None
    i = pl.program_id(1)
    if cast:
        (w_bf,) = refs

        @pl.when(i == 0)
        def _():
            w_bf[...] = w_ref[0].astype(BF16)
    else:
        w_bf = w_ref.at[0]

    def run(src_ref, dst_ref):
        def body():
            for rs in _row_blocks(src_ref.shape[0], sub):
                t = _dot_nt(src_ref[rs, :], w_bf[...])
                if biased:
                    t = t + b_ref[...]
                dst_ref[rs, :] = act(t).astype(dst_ref.dtype)
        return body

    if extra:
        pl.when(i < n_main)(run(h_ref, o_ref))
        pl.when(i == n_main)(run(hx_ref, ox_ref))
    else:
        run(h_ref, o_ref)()


def _proj(h, w_t, layer, col0, width, act, out_dtype, tm, tn, bias=None, extra=None):
    m, d = h.shape
    cast = w_t.dtype != BF16
    n_main = m // tm
    osz = jnp.dtype(out_dtype).itemsize
    est = 2 * (tm * d * 2 + d * tn * w_t.dtype.itemsize + tm * tn * osz + tn * 4) + d * tn * 2 + 256 * tn * 8
    last = n_main - 1
    in_specs = [pl.BlockSpec((tm, d), lambda j, i: (jnp.minimum(i, last), 0))]
    args = [h]
    out_specs = [pl.BlockSpec((tm, tn), lambda j, i: (jnp.minimum(i, last), j))]
    out_shape = [jax.ShapeDtypeStruct((m, width), out_dtype)]
    if extra is not None:
        hx, xdtype = extra
        mx = hx.shape[0]
        in_specs.append(pl.BlockSpec((mx, d), lambda j, i: (0, 0)))
        args.append(hx)
        out_specs.append(pl.BlockSpec((mx, tn), lambda j, i: (0, j)))
        out_shape.append(jax.ShapeDtypeStruct((mx, width), xdtype))
        est += 2 * (mx * d * 2 + mx * tn * jnp.dtype(xdtype).itemsize)
    in_specs.append(pl.BlockSpec((pl.Element(1), pl.Element(tn), pl.Element(d)),
                                 lambda j, i: (layer, pl.multiple_of(col0 + j * tn, SUBLANES), 0)))
    args.append(w_t)
    if bias is not None:
        in_specs.append(_lspec((1, tn), layer, lambda j, i: (0, j)))
        args.append(bias)
    outs = pl.pallas_call(
        functools.partial(_proj_body, act=act, sub=256, cast=cast, biased=bias is not None,
                          extra=extra is not None, n_main=n_main),
        grid=(width // tn, n_main + (extra is not None)),
        in_specs=in_specs,
        out_specs=out_specs,
        out_shape=out_shape,
        scratch_shapes=[pltpu.VMEM((tn, d), BF16)] if cast else [],
        compiler_params=pltpu.CompilerParams(
            dimension_semantics=("arbitrary", "arbitrary"), vmem_limit_bytes=_vmem_limit(est)),
        name="proj",
    )(*args)
    return outs if extra is not None else outs[0]


def _identity(t):
    return t


def _sgu_body(u_ref, v_ref, lng_ref, lnb_ref, mix_ref, bias_ref, wa_ref, *rest, emit_v, chained):
    if emit_v:
        if chained:
            rest = rest[1:]
        a_ref, vout_ref, ya_scr = rest
    else:
        a_ref, ya_scr = rest
    vg = v_ref[...].astype(F32)
    mu = jnp.mean(vg, axis=-1, keepdims=True)
    var = jnp.mean(jnp.square(vg - mu), axis=-1, keepdims=True)
    v = (vg - mu) * lax.rsqrt(var + EPS) * lng_ref[...] + lnb_ref[...]
    if emit_v:
        vout_ref[...] = v
    vb = v.astype(BF16)
    rows, width = vb.shape
    for c in range(rows // SGU_CHUNK):
        rs = slice(c * SGU_CHUNK, (c + 1) * SGU_CHUNK)
        for grp in range(width // SGU_GROUP_DIM):
            cs = slice(grp * SGU_GROUP_DIM, (grp + 1) * SGU_GROUP_DIM)
            mixed = _dot(mix_ref[grp], vb[rs, cs]) + bias_ref[:, cs]
            ya_scr[rs, cs] = (u_ref[rs, cs].astype(F32) * mixed).astype(BF16)
    a_ref[...] = _dot(ya_scr[...], wa_ref[...]).astype(a_ref.dtype)


def _sgu(su, lng, lnb, mix, bias, w_out_a, layer, rows, emit_v, prev=None):
    m = su.shape[0]
    width = su.shape[1] // 2
    depth, _, d = w_out_a.shape
    ngrp = mix.shape[1]
    isz = su.dtype.itemsize
    est = 2 * (2 * rows * width * isz + ngrp * SGU_CHUNK * SGU_CHUNK * 2 + SGU_CHUNK * width * 4
               + rows * d * 2 + rows * width * 4) + width * d * 2 + 4 * rows * width * 4
    out_shape = [jax.ShapeDtypeStruct((m, d), BF16)]
    out_specs = [pl.BlockSpec((rows, d), lambda i: (i, 0))]
    if emit_v:
        out_shape.append(jax.ShapeDtypeStruct((depth, m, width), F32))
        out_specs.append(pl.BlockSpec((None, rows, width), lambda i: (layer, i, 0)))
    in_specs = [
        pl.BlockSpec((rows, width), lambda i: (i, 0)),
        pl.BlockSpec((rows, width), lambda i: (i, 1)),
        _lspec((1, width), layer, lambda i: (0, 0)),
        _lspec((1, width), layer, lambda i: (0, 0)),
        _lspec((ngrp, SGU_CHUNK, SGU_CHUNK), layer, lambda i: (0, 0, 0)),
        _lspec((SGU_CHUNK, width), layer, lambda i: (0, 0)),
        pl.BlockSpec((None, width, d), lambda i: (layer, 0, 0), pipeline_mode=pl.Buffered(1)),
    ]
    args = [su, su, lng, lnb, mix, bias, w_out_a]
    aliases = _chain(args, in_specs, prev, 1)
    return pl.pallas_call(
        functools.partial(_sgu_body, emit_v=emit_v, chained=prev is not None),
        grid=(m // rows,),
        in_specs=in_specs,
        out_specs=out_specs,
        out_shape=out_shape,
        scratch_shapes=[pltpu.VMEM((rows, width), BF16)],
        input_output_aliases=aliases,
        compiler_params=pltpu.CompilerParams(
            dimension_semantics=("arbitrary",), vmem_limit_bytes=_vmem_limit(est)),
        name="sgu",
    )(*args)


def _prefix_sum_rows(x, row):
    n = x.shape[0]
    s = 1
    while s < n:
        x = x + jnp.where(row >= s, pltpu.roll(x, s, 0), 0.0)
        s *= 2
    return x


def _gate_groupnorm(y, z, ng):
    t = y * _silu(z)
    return t * lax.rsqrt(jnp.mean(t * t, axis=-1, keepdims=True) + EPS) * ng


def _split_hi_lo(v):
    hi = v.astype(BF16)
    lo = (v - hi.astype(F32)).astype(BF16)
    return jnp.concatenate([hi, lo], axis=1)


def _ssd_prompt_body(xbc_ref, z_ref, dt_ref, cw_ref, cb_ref, alog_ref, dexp_ref, ng_ref, e2_ref, shift_ref, *rest,
                     d_inner, n_chunks, chained):
    if chained:
        rest = rest[2:]
    yb_ref, h_ref, cst_ref, win_scr, act_scr, y_scr, ht_scr, cst_scr = rest
    c = pl.program_id(1)
    q = xbc_ref.shape[0]
    conv_dim = xbc_ref.shape[1]
    p = SSM_HEAD_DIM
    n_state = SSM_STATE
    gw = d_inner // SSM_GROUPS
    pairs_per_group = gw // (2 * p)
    halo = CONV_HALO

    @pl.when(c == 0)
    def _():
        win_scr[...] = jnp.zeros(win_scr.shape, BF16)
        ht_scr[...] = jnp.zeros(ht_scr.shape, F32)

    win_scr[halo:halo + q, :] = xbc_ref[...]
    lane_chunk = 512
    for j in range(conv_dim // lane_chunk):
        ls = slice(j * lane_chunk, (j + 1) * lane_chunk)
        shifted = _dot(shift_ref[...], win_scr[:, ls])
        acc = cb_ref[:, ls] + cw_ref[CONV_WIDTH - 1:CONV_WIDTH, ls] * xbc_ref[:, ls].astype(F32)
        for k in range(CONV_WIDTH - 1):
            acc = acc + cw_ref[k:k + 1, ls] * shifted[k * q:(k + 1) * q, :]
        act_scr[:, ls] = _silu(acc)

    @pl.when(c == n_chunks - 1)
    def _():
        tail = xbc_ref[q - halo:q, :].astype(F32)
        cst_ref[...] = tail[halo - (CONV_WIDTH - 1):halo, :]

    win_scr[0:halo, :] = xbc_ref[q - halo:q, :]

    row = lax.broadcasted_iota(jnp.int32, (q, LANES), 0)
    lane = lax.broadcasted_iota(jnp.int32, (q, LANES), 1)
    dt = dt_ref[...]
    da = dt * (-jnp.exp(alog_ref[...]))
    cs = _prefix_sum_rows(da, row)
    cs2 = cs * LOG2_E
    cst_scr[...] = cs2.T
    dt_hl = _split_hi_lo(dt)
    ecs_hl = _split_hi_lo(jnp.exp(cs))
    dec_hl = _split_hi_lo(jnp.exp(cs[q - 1:q, :] - cs))
    causal = row >= lane
    lower_half = lane < p

    for grp in range(SSM_GROUPS):
        gs = slice(grp * gw, (grp + 1) * gw)
        e2 = e2_ref[:, gs]
        dt_g = _dot(dt_hl, e2)
        ecs_g = _dot(ecs_hl, e2)
        dec_g = _dot(dec_hl, e2)
        xs_g = act_scr[:, gs]
        xdt_g = xs_g * dt_g
        b_g = act_scr[:, d_inner + grp * n_state:d_inner + (grp + 1) * n_state].astype(BF16)
        c_g = act_scr[:, d_inner + (SSM_GROUPS + grp) * n_state:
                      d_inner + (SSM_GROUPS + grp + 1) * n_state].astype(BF16)
        cbm = jnp.where(causal, _dot_nt(c_g, b_g), 0.0)
        ht_g = ht_scr[:, gs]
        y_g = _dot(c_g, ht_g.astype(BF16)) * ecs_g + dexp_ref[:, gs] * xs_g
        for j in range(pairs_per_group):
            h0 = grp * 2 * pairs_per_group + 2 * j
            ws = []
            for h in (h0, h0 + 1):
                seg2 = cs2[:, h:h + 1] - cst_scr[h:h + 1, :]
                ws.append((cbm * jnp.exp2(jnp.where(causal, seg2, 0.0))).astype(BF16))
            xpair = xdt_g[:, j * LANES:(j + 1) * LANES]
            rhs = jnp.concatenate([jnp.where(lower_half, xpair, 0.0).astype(BF16),
                                   jnp.where(lower_half, 0.0, xpair).astype(BF16)], axis=0)
            ls = slice(grp * gw + j * LANES, grp * gw + (j + 1) * LANES)
            y_scr[:, ls] = y_g[:, j * LANES:(j + 1) * LANES] + _dot(jnp.concatenate(ws, axis=1), rhs)
        states_t = _dot_tn(b_g, (xdt_g * dec_g).astype(BF16))
        ht_scr[:, gs] = ht_g * ecs_g[q - 1:q, :] + states_t

    for grp in range(SSM_GROUPS):
        gs = slice(grp * gw, (grp + 1) * gw)
        yb_ref[:, gs] = _gate_groupnorm(y_scr[:, gs], z_ref[:, gs].astype(F32), ng_ref[:, gs]).astype(yb_ref.dtype)

    @pl.when(c == n_chunks - 1)
    def _():
        for grp in range(SSM_GROUPS):
            gs = slice(grp * gw, (grp + 1) * gw)
            h_ref[gs, :] = ht_scr[:, gs].T


def _ssd_prompt(xbc, z, dt, conv_w, conv_b, a_log, d_exp, norm_g, e2, shift, batch, seq, depth, layer, prev):
    q = SSD_CHUNK
    n_chunks = seq // q
    conv_dim = xbc.shape[1]
    d_inner = z.shape[1]
    dtw = dt.shape[1]
    est = (2 * (q * conv_dim * 2 + q * d_inner * 2 + q * dtw * 4 + q * d_inner * 2 + d_inner * SSM_STATE * 4
                + 8 * conv_dim * 4 + 5 * conv_dim * 4 + 2 * d_inner * 4 + 2 * LANES * d_inner * 2)
           + (q + 16) * conv_dim * 4 + q * conv_dim * 4 + 2 * q * d_inner * 4 + (8 << 20))
    row_map = lambda b, c: (b * n_chunks + c, 0)
    const = lambda b, c: (0, 0)
    in_specs = [
        pl.BlockSpec((q, conv_dim), row_map),
        pl.BlockSpec((q, d_inner), row_map),
        pl.BlockSpec((q, dtw), row_map),
        _lspec((CONV_WIDTH, conv_dim), layer, const),
        _lspec((1, conv_dim), layer, const),
        _lspec((1, dtw), layer, const),
        _lspec((1, d_inner), layer, const),
        _lspec((1, d_inner), layer, const),
        pl.BlockSpec((2 * LANES, d_inner), const),
        pl.BlockSpec(shift.shape, const),
    ]
    args = [xbc, z, dt, conv_w, conv_b, a_log, d_exp, norm_g, e2, shift]
    aliases = _chain(args, in_specs, prev, 1)
    return pl.pallas_call(
        functools.partial(_ssd_prompt_body, d_inner=d_inner, n_chunks=n_chunks, chained=prev is not None),
        grid=(batch, n_chunks),
        in_specs=in_specs,
        out_specs=[
            pl.BlockSpec((q, d_inner), row_map),
            pl.BlockSpec((None, None, d_inner, SSM_STATE), lambda b, c: (layer, b, 0, 0)),
            pl.BlockSpec((None, None, CONV_WIDTH - 1, conv_dim), lambda b, c: (layer, b, 0, 0)),
        ],
        out_shape=[
            jax.ShapeDtypeStruct((batch * seq, d_inner), BF16),
            jax.ShapeDtypeStruct((depth, batch, d_inner, SSM_STATE), F32),
            jax.ShapeDtypeStruct((depth, batch, CONV_WIDTH - 1, conv_dim), F32),
        ],
        scratch_shapes=[
            pltpu.VMEM((2 * q, conv_dim), BF16),
            pltpu.VMEM((q, conv_dim), F32),
            pltpu.VMEM((q, d_inner), F32),
            pltpu.VMEM((SSM_STATE, d_inner), F32),
            pltpu.VMEM((LANES, q), F32),
        ],
        input_output_aliases=aliases,
        compiler_params=pltpu.CompilerParams(
            dimension_semantics=("arbitrary", "arbitrary"), vmem_limit_bytes=_vmem_limit(est)),
        name="ssd_prompt",
    )(*args)


def _ssd_sample_body(xbc_ref, z_ref, dte_ref, cbuf_ref, h0_ref, cw_ref, cb_ref, aexp_ref, dexp_ref, ng_ref,
                     *rest, d_inner, steps, chained):
    if chained:
        rest = rest[2:]
    yb_ref, h_ref, cst_ref, win_scr = rest
    q = xbc_ref.shape[0]
    nseq = q // steps
    conv_dim = xbc_ref.shape[1]
    n_state = SSM_STATE
    gw = d_inner // SSM_GROUPS
    halo = SUBLANES
    span = 2 * SUBLANES

    win_scr[...] = jnp.zeros(win_scr.shape, F32)
    for j in range(nseq):
        win_scr[j * span + halo - 3:j * span + halo, :] = cbuf_ref[j]
        win_scr[j * span + halo:j * span + halo + steps, :] = xbc_ref[j * steps:(j + 1) * steps, :]
        cst_ref[j] = win_scr[j * span + halo + steps - 3:j * span + halo + steps, :]
    rowc = lax.broadcasted_iota(jnp.int32, (q, conv_dim), 0)
    acc = None
    for j in range(nseq):
        base = j * span + halo - 3 - j * steps
        acc_j = cb_ref[...] + cw_ref[0:1, :] * win_scr[base:base + q, :]
        for k in range(1, CONV_WIDTH):
            acc_j = acc_j + cw_ref[k:k + 1, :] * win_scr[base + k:base + k + q, :]
        acc = acc_j if acc is None else jnp.where(rowc >= j * steps, acc_j, acc)
    act = _silu(acc)

    xs = act[:, :d_inner]
    row = lax.broadcasted_iota(jnp.int32, (q, d_inner), 0)
    tok = row % steps
    dt = dte_ref[...]
    da = dt * aexp_ref[...]
    cs = da
    s = 1
    while s < steps:
        cs = cs + jnp.where(tok >= s, pltpu.roll(cs, s, 0), 0.0)
        s *= 2
    cs_last = cs[steps - 1:steps, :]
    for j in range(1, nseq):
        cs_last = jnp.where(row >= j * steps, cs[(j + 1) * steps - 1:(j + 1) * steps, :], cs_last)
    xdt = xs * dt

    b_all = act[:, d_inner:d_inner + SSM_GROUPS * n_state]
    c_all = act[:, d_inner + SSM_GROUPS * n_state:]
    y = dexp_ref[...] * xs
    for o in range(steps):
        b_sh = b_all if o == 0 else pltpu.roll(b_all, o, 0)
        cs_sh = cs if o == 0 else pltpu.roll(cs, o, 0)
        xdt_sh = xdt if o == 0 else pltpu.roll(xdt, o, 0)
        prod = c_all * b_sh
        cb_o = jnp.concatenate(
            [jnp.broadcast_to(jnp.sum(prod[:, grp * n_state:(grp + 1) * n_state], axis=1, keepdims=True), (q, gw))
             for grp in range(SSM_GROUPS)], axis=1)
        keep = tok >= o
        lmat = jnp.where(keep, jnp.exp(jnp.where(keep, cs - cs_sh, 0.0)), 0.0)
        y = y + cb_o * lmat * xdt_sh

    ecs = jnp.exp(cs)
    xs_dec = xdt * jnp.exp(cs_last - cs)
    cd = jnp.exp(cs_last)
    rowg = lax.broadcasted_iota(jnp.int32, (q, gw), 0)
    lane = lax.broadcasted_iota(jnp.int32, (q, LANES), 1)
    p = SSM_HEAD_DIM
    for grp in range(SSM_GROUPS):
        gs = slice(grp * gw, (grp + 1) * gw)
        b_g = b_all[:, grp * n_state:(grp + 1) * n_state].astype(BF16)
        c_g = c_all[:, grp * n_state:(grp + 1) * n_state].astype(BF16)
        head_cd = []
        for t in range(gw // LANES):
            v = cd[:, grp * gw + t * LANES:grp * gw + (t + 1) * LANES]
            vr = pltpu.roll(v, p, 1)
            head_cd += [jnp.where(lane < p, v, vr), jnp.where(lane < p, vr, v)]
        y_off = None
        for j in range(nseq):
            h_g = h0_ref[j, gs, :]
            mine = (rowg >= j * steps) & (rowg < (j + 1) * steps)
            y_off_j = _dot_nt(c_g, h_g.astype(BF16))
            y_off = y_off_j if y_off is None else jnp.where(mine, y_off_j, y_off)
            states = _dot_tn(jnp.where(mine, xs_dec[:, gs], 0.0).astype(BF16), b_g)
            scale = jnp.concatenate(
                [jnp.broadcast_to(hc[j * steps:j * steps + 1, :], (p, n_state)) for hc in head_cd], axis=0)
            h_ref[j, gs, :] = h_g * scale + states
        y_g = y[:, gs] + y_off * ecs[:, gs]
        yb_ref[:, gs] = _gate_groupnorm(y_g, z_ref[:, gs].astype(F32), ng_ref[:, gs]).astype(yb_ref.dtype)


def _ssd_sample(xbc, z, dte, conv_state, h0_all, layer, conv_w, conv_b, a_exp, d_exp, norm_g, steps, prev):
    m, conv_dim = xbc.shape
    d_inner = z.shape[1]
    depth, nseq_total = h0_all.shape[:2]
    q = SUBLANES
    nseq = q // steps
    est = (2 * (2 * nseq * d_inner * SSM_STATE * 4 + q * (conv_dim + 3 * d_inner) * 4
                + 2 * nseq * SUBLANES * conv_dim * 4 + 5 * conv_dim * 4 + 3 * d_inner * 4)
           + 2 * nseq * SUBLANES * conv_dim * 4 + d_inner * LANES * 4 + (12 << 20))
    row_map = lambda s: (s, 0)
    lay_map = lambda s: (layer, s, 0, 0)
    const = lambda s: (0, 0)
    in_specs = [
        pl.BlockSpec((q, conv_dim), row_map),
        pl.BlockSpec((q, d_inner), row_map),
        pl.BlockSpec((q, d_inner), row_map),
        pl.BlockSpec((None, nseq, CONV_WIDTH - 1, conv_dim), lay_map),
        pl.BlockSpec((None, nseq, d_inner, SSM_STATE), lay_map),
        _lspec((CONV_WIDTH, conv_dim), layer, const),
        _lspec((1, conv_dim), layer, const),
        _lspec((1, d_inner), layer, const),
        _lspec((1, d_inner), layer, const),
        _lspec((1, d_inner), layer, const),
    ]
    args = [xbc, z, dte, conv_state, h0_all, conv_w, conv_b, a_exp, d_exp, norm_g]
    aliases = _chain(args, in_specs, prev, 1)
    return pl.pallas_call(
        functools.partial(_ssd_sample_body, d_inner=d_inner, steps=steps, chained=prev is not None),
        grid=(m // q,),
        in_specs=in_specs,
        out_specs=[
            pl.BlockSpec((q, d_inner), row_map),
            pl.BlockSpec((None, nseq, d_inner, SSM_STATE), lay_map),
            pl.BlockSpec((None, nseq, CONV_WIDTH - 1, conv_dim), lay_map),
        ],
        out_shape=[
            jax.ShapeDtypeStruct((m, d_inner), F32),
            jax.ShapeDtypeStruct((depth, nseq_total, d_inner, SSM_STATE), F32),
            jax.ShapeDtypeStruct((depth, nseq_total, CONV_WIDTH - 1, conv_dim), F32),
        ],
        scratch_shapes=[pltpu.VMEM((nseq * 2 * SUBLANES, conv_dim), F32)],
        input_output_aliases=aliases,
        compiler_params=pltpu.CompilerParams(
            dimension_semantics=("arbitrary",), vmem_limit_bytes=_vmem_limit(est)),
        name="ssd_sample",
    )(*args)


def _merge_body(a_ref, yb_ref, ga_ref, gb_ref, x_ref, wb_ref, wo_ref, g_ref, o_ref, h_ref):
    b = _dot(yb_ref[...].astype(BF16), wb_ref[...])
    merged = ga_ref[...].astype(F32) * a_ref[...].astype(F32) + gb_ref[...].astype(F32) * b
    x1 = x_ref[...] + _dot(merged.astype(BF16), wo_ref[...])
    o_ref[...] = x1
    h_ref[...] = _rmsnorm(x1, g_ref[...]).astype(h_ref.dtype)


def _merge(a, yb, gates, x, w_out_b, w_o, norm_g, layer, rows):
    m, d = x.shape
    d_inner = yb.shape[1]
    est = (2 * rows * (d * 2 + d_inner * yb.dtype.itemsize + 2 * d * gates.dtype.itemsize + 2 * d * 4 + d * 2)
           + d_inner * d * 2 + d * d * 2 + 3 * rows * d * 4)
    return pl.pallas_call(
        _merge_body,
        grid=(m // rows,),
        in_specs=[
            pl.BlockSpec((rows, d), lambda i: (i, 0)),
            pl.BlockSpec((rows, d_inner), lambda i: (i, 0)),
            pl.BlockSpec((rows, d), lambda i: (i, 0)),
            pl.BlockSpec((rows, d), lambda i: (i, 1)),
            pl.BlockSpec((rows, d), lambda i: (i, 0)),
            pl.BlockSpec((None, d_inner, d), lambda i: (layer, 0, 0), pipeline_mode=pl.Buffered(1)),
            pl.BlockSpec((None, d, d), lambda i: (layer, 0, 0), pipeline_mode=pl.Buffered(1)),
            _lspec((1, d), layer, lambda i: (0, 0)),
        ],
        out_specs=[pl.BlockSpec((rows, d), lambda i: (i, 0)), pl.BlockSpec((rows, d), lambda i: (i, 0))],
        out_shape=[jax.ShapeDtypeStruct((m, d), F32), jax.ShapeDtypeStruct((m, d), BF16)],
        compiler_params=pltpu.CompilerParams(
            dimension_semantics=("arbitrary",), vmem_limit_bytes=_vmem_limit(est)),
        name="merge",
    )(a, yb, gates, gates, x, w_out_b, w_o, norm_g)


def _ffn_body(x_ref, h_ref, wg_ref, wu_ref, wd_ref, g_ref, o_ref, *rest, n_f, final, sub):
    f = pl.program_id(1)
    blocks = _row_blocks(x_ref.shape[0], sub)

    @pl.when(f == 0)
    def _():
        for rs in blocks:
            o_ref[rs, :] = x_ref[rs, :]

    for rs in blocks:
        hb = h_ref[rs, :]
        act = _silu(_dot(hb, wg_ref[...])) * _dot(hb, wu_ref[...])
        o_ref[rs, :] += _dot(act.astype(BF16), wd_ref[...])

    @pl.when(f == n_f - 1)
    def _():
        for rs in blocks:
            y = _rmsnorm(o_ref[rs, :], g_ref[...])
            if final:
                o_ref[rs, :] = y
            else:
                rest[0][rs, :] = y.astype(BF16)


def _ffn(x, h, w_gate, w_up, w_down, next_g, layer, g_layer, tm, final):
    m, d = x.shape
    n_f, tf = w_gate.shape[1], w_gate.shape[3]
    est = tm * d * 6 + 2 * (tm * d * 4 + tm * d * 2 + 3 * d * tf * 2) + 256 * (2 * tf + d) * 4
    out_specs = [pl.BlockSpec((tm, d), lambda i, j: (i, 0))]
    out_shape = [jax.ShapeDtypeStruct((m, d), F32)]
    if not final:
        out_specs.append(pl.BlockSpec((tm, d), lambda i, j: (i, 0)))
        out_shape.append(jax.ShapeDtypeStruct((m, d), BF16))
    return pl.pallas_call(
        functools.partial(_ffn_body, n_f=n_f, final=final, sub=256),
        grid=(m // tm, n_f),
        in_specs=[
            pl.BlockSpec((tm, d), lambda i, j: (i, 0), pipeline_mode=pl.Buffered(1)),
            pl.BlockSpec((tm, d), lambda i, j: (i, 0), pipeline_mode=pl.Buffered(1)),
            _lspec((None, d, tf), layer, lambda i, j: (j, 0, 0)),
            _lspec((None, d, tf), layer, lambda i, j: (j, 0, 0)),
            _lspec((tf, d), layer, lambda i, j: (j, 0)),
            _lspec((1, d), g_layer, lambda i, j: (0, 0)),
        ],
        out_specs=out_specs,
        out_shape=out_shape,
        compiler_params=pltpu.CompilerParams(
            dimension_semantics=("arbitrary", "arbitrary"), vmem_limit_bytes=_vmem_limit(est)),
        name="ffn",
    )(x, h, w_gate, w_up, w_down, next_g)


def _row_tile(m, target):
    t = min(m, target)
    while m % t:
        t //= 2
    return t


def kernel(x_prompt, x_sample, state_ssm, state_conv, norm1_g, w_in, conv_w, conv_b, dt_bias, a_log, d_skip, ssm_norm_g, sgu_ln_g, sgu_ln_b, sgu_w, sgu_b, w_out_a, w_out_b, w_o, norm2_g, w_ffn_gate, w_ffn_up, w_ffn_down, final_norm_g):
    batch, seq, d = x_prompt.shape
    nseq, steps, _ = x_sample.shape
    depth = w_in.shape[0]
    heads = dt_bias.shape[1]
    d_inner = heads * SSM_HEAD_DIM
    conv_dim = conv_w.shape[2]
    sgu_width = sgu_ln_g.shape[1]
    ngrp = sgu_w.shape[1]
    seg_widths = (2 * sgu_width, d_inner, conv_dim, 2 * d)
    dt_col = 2 * sgu_width + d_inner + conv_dim

    xp = x_prompt.reshape(batch * seq, d)
    xs = x_sample.reshape(nseq * steps, d)
    state_ssm_flat = state_ssm.reshape(depth, nseq, d_inner, SSM_STATE)

    def rows3(p):
        return p.reshape(depth, 1, p.shape[-1])

    w_in_t = jnp.swapaxes(w_in, 1, 2)
    dt_b = rows3(jnp.pad(dt_bias, ((0, 0), (0, LANES - heads))))
    w_dt_exp_t = jnp.repeat(w_in_t[:, dt_col:dt_col + heads, :], SSM_HEAD_DIM, axis=1).astype(BF16)
    dt_b_exp = rows3(jnp.repeat(dt_bias, SSM_HEAD_DIM, axis=1))
    a_log_pad = rows3(jnp.pad(a_log, ((0, 0), (0, LANES - heads))))
    a_exp = rows3(jnp.repeat(-jnp.exp(a_log), SSM_HEAD_DIM, axis=1))
    d_exp = rows3(jnp.repeat(d_skip, SSM_HEAD_DIM, axis=1))
    n1, n2, ng = rows3(norm1_g), rows3(norm2_g), rows3(ssm_norm_g)
    cb, lng, lnb = rows3(conv_b), rows3(sgu_ln_g), rows3(sgu_ln_b)
    tril = jnp.tril(jnp.ones((SGU_CHUNK, SGU_CHUNK), bool))
    mix_p = jnp.where(tril, sgu_w, 0).astype(BF16)
    bias_p = jnp.repeat(jnp.swapaxes(sgu_b, 1, 2), SGU_GROUP_DIM, axis=2)
    seqs_per_chunk = SGU_CHUNK // steps
    eye = jnp.eye(seqs_per_chunk, dtype=F32)
    w_small = jnp.where(jnp.tril(jnp.ones((steps, steps), bool)), sgu_w[:, :, :steps, :steps], 0)
    mix_s = jnp.einsum("ab,lgts->lgatbs", eye, w_small).reshape(depth, ngrp, SGU_CHUNK, SGU_CHUNK).astype(BF16)
    bias_s = jnp.tile(bias_p[:, :steps], (1, seqs_per_chunk, 1))
    wa, wb, wo = w_out_a.astype(BF16), w_out_b.astype(BF16), w_o.astype(BF16)
    def hidden_tiles(w):
        return jnp.swapaxes(w.astype(BF16).reshape(depth, d, -1, FFN_TILE), 1, 2)

    wg, wu, wd = hidden_tiles(w_ffn_gate), hidden_tiles(w_ffn_up), w_ffn_down.astype(BF16)
    expand = (jnp.arange(d_inner)[None, :] // SSM_HEAD_DIM == jnp.arange(LANES)[:, None]).astype(BF16)
    e2 = jnp.concatenate([expand, expand], axis=0)
    rq = jnp.arange((CONV_WIDTH - 1) * SSD_CHUNK)
    shift = (jnp.arange(2 * SSD_CHUNK)[None, :]
             == (CONV_HALO - (CONV_WIDTH - 1) + rq % SSD_CHUNK + rq // SSD_CHUNK)[:, None]).astype(BF16)

    fin_g3 = final_norm_g.reshape(1, 1, d)
    w_su, w_z, w_xbc, w_gt = seg_widths

    def project(h, h_sample, l, tm):
        segs = ((0, w_su, jax.nn.gelu), (w_su, w_z, _identity), (w_su + w_z, w_xbc, _identity),
                (dt_col + heads, w_gt, jax.nn.sigmoid))
        outs = [_proj(h, w_in_t, l, c0, wdt, act, BF16, tm, 1024, extra=(h_sample, F32)) for c0, wdt, act in segs]
        return [o[0] for o in outs], [o[1] for o in outs]

    def finish(x, a, yb, gt, l, tm):
        x, h2 = _merge(a, yb, gt, x, wb, wo, n2, l, _row_tile(x.shape[0], 256))
        if l == depth - 1:
            (y,) = _ffn(x, h2, wg, wu, wd, fin_g3, l, 0, tm, True)
            return y, None
        return _ffn(x, h2, wg, wu, wd, n1, l, l + 1, tm, False)

    tm_p = _row_tile(xp.shape[0], 1024)
    tm_s = _row_tile(xs.shape[0], 512)
    hp = _norm(xp, n1, 0, tm_p)
    hs = _norm(xs, n1, 0, tm_s)
    prev_p = prev_s = prev_v = None
    for l in range(depth):
        (su, z, xbc, gt), sample_proj = project(hp, hs, l, tm_p)

        dt = _proj(hp, w_in_t, l, dt_col, LANES, _softplus, F32, tm_p, LANES, bias=dt_b)
        (a,) = _sgu(su, lng, lnb, mix_p, bias_p, wa, l, _row_tile(xp.shape[0], 512), False)
        yb, *prev_p = _ssd_prompt(xbc, z, dt, conv_w, cb, a_log_pad, d_exp, ng, e2, shift, batch, seq, depth, l,
                                  prev_p)
        xp, hp = finish(xp, a, yb, gt, l, tm_p)

        su, z, xbc, gt = sample_proj
        dte = _proj(hs, w_dt_exp_t, l, 0, d_inner, _softplus, F32, tm_s, 1024, bias=dt_b_exp)
        a, *prev_v = _sgu(su, lng, lnb, mix_s, bias_s, wa, l, tm_s, True, prev_v)
        yb, *prev_s = _ssd_sample(xbc, z, dte, state_conv, state_ssm_flat, l, conv_w, cb, a_exp, d_exp, ng,
                                  steps, prev_s)
        xs, hs = finish(xs, a, yb, gt, l, tm_s)

    ssm_p, conv_p = prev_p
    ssm_s, conv_s = prev_s
    (v_s,) = prev_v
    return (xp.reshape(batch, seq, d), xs.reshape(nseq, steps, d),
            ssm_p.reshape(depth, batch, heads, SSM_HEAD_DIM, SSM_STATE), conv_p,
            ssm_s.reshape(depth, nseq, heads, SSM_HEAD_DIM, SSM_STATE), conv_s,
            v_s.reshape(depth, nseq, steps, sgu_width))
```

```python
import functools

import jax
import jax.numpy as jnp
from jax import lax
from jax.experimental import pallas as pl
from jax.experimental.pallas import tpu as pltpu

F32 = jnp.float32
BF16 = jnp.bfloat16
EPS = 1e-6
LOG2_E = 1.4426950408889634

LANES = 128
SUBLANES = 8
VMEM_CAP_BYTES = 64 * 1024 * 1024

SGU_CHUNK = 128
SGU_GROUP_DIM = 128
SSM_HEAD_DIM = 64
SSM_STATE = 128
SSM_GROUPS = 8
CONV_WIDTH = 4
CONV_HALO = 16
SSD_CHUNK = 128


def _vmem_limit(estimate_bytes):
    return int(min(estimate_bytes + (8 << 20), VMEM_CAP_BYTES - (6 << 20)))


def _silu(x):
    return x * jax.nn.sigmoid(x)


def _softplus(x):
    return jnp.maximum(x, 0.0) + jnp.log1p(jnp.exp(-jnp.abs(x)))


def _rmsnorm(x, g):
    return x * lax.rsqrt(jnp.mean(x * x, axis=-1, keepdims=True) + EPS) * g


def _dot(a, b):
    return jnp.dot(a, b, preferred_element_type=F32)


def _dot_nt(a, b):
    return lax.dot_general(a, b, (((1,), (1,)), ((), ())), preferred_element_type=F32)


def _dot_tn(a, b):
    return lax.dot_general(a, b, (((0,), (0,)), ((), ())), preferred_element_type=F32)


def _chain(args, in_specs, prev, first_out):
    aliases = {}
    if prev is not None:
        for k, arr in enumerate(prev):
            aliases[len(args)] = first_out + k
            args.append(arr)
            in_specs.append(pl.BlockSpec(memory_space=pl.ANY))
    return aliases


def _row_blocks(rows, sub):
    sub = min(rows, sub)
    return [slice(r * sub, (r + 1) * sub) for r in range(rows // sub)]


def _lspec(block, layer, tail_index):
    return pl.BlockSpec((None,) + tuple(block), lambda *ids: (layer,) + tuple(tail_index(*ids)))


def _norm_body(x_ref, g_ref, o_ref):
    o_ref[...] = _rmsnorm(x_ref[...], g_ref[...]).astype(o_ref.dtype)


def _norm(x, g, layer, rows):
    m, d = x.shape
    est = 2 * rows * d * 6 + rows * d * 8
    return pl.pallas_call(
        _norm_body,
        grid=(m // rows,),
        in_specs=[pl.BlockSpec((rows, d), lambda i: (i, 0)), _lspec((1, d), layer, lambda i: (0, 0))],
        out_specs=pl.BlockSpec((rows, d), lambda i: (i, 0)),
        out_shape=jax.ShapeDtypeStruct((m, d), BF16),
        compiler_params=pltpu.CompilerParams(
            dimension_semantics=("arbitrary",), vmem_limit_bytes=_vmem_limit(est)),
        name="norm",
    )(x, g)


def _proj_body(*refs, act, sub, cast, biased, extra, n_main):
    refs = list(refs)
    h_ref = refs.pop(0)
    hx_ref = refs.pop(0) if extra else None
    w_ref = refs.pop(0)
    b_ref = refs.pop(0) if biased else None
    o_ref = refs.pop(0)
    ox_ref = refs.pop(0) if extra else None
    i = pl.program_id(1)
    if cast:
        (w_bf,) = refs

        @pl.when(i == 0)
        def _():
            w_bf[...] = w_ref[0].astype(BF16)
    else:
        w_bf = w_ref.at[0]

    def run(src_ref, dst_ref):
        def body():
            for rs in _row_blocks(src_ref.shape[0], sub):
                t = _dot_nt(src_ref[rs, :], w_bf[...])
                if biased:
                    t = t + b_ref[...]
                dst_ref[rs, :] = act(t).astype(dst_ref.dtype)
        return body

    if extra:
        pl.when(i < n_main)(run(h_ref, o_ref))
        pl.when(i == n_main)(run(hx_ref, ox_ref))
    else:
        run(h_ref, o_ref)()


def _proj(h, w_t, layer, col0, width, act, out_dtype, tm, tn, bias=None, extra=None):
    m, d = h.shape
    cast = w_t.dtype != BF16
    n_main = m // tm
    osz = jnp.dtype(out_dtype).itemsize
    est = 2 * (tm * d * 2 + d * tn * w_t.dtype.itemsize + tm * tn * osz + tn * 4) + d * tn * 2 + 256 * tn * 8
    last = n_main - 1
    in_specs = [pl.BlockSpec((tm, d), lambda j, i: (jnp.minimum(i, last), 0))]
    args = [h]
    out_specs = [pl.BlockSpec((tm, tn), lambda j, i: (jnp.minimum(i, last), j))]
    out_shape = [jax.ShapeDtypeStruct((m, width), out_dtype)]
    if extra is not None:
        hx, xdtype = extra
        mx = hx.shape[0]
        in_specs.append(pl.BlockSpec((mx, d), lambda j, i: (0, 0)))
        args.append(hx)
        out_specs.append(pl.BlockSpec((mx, tn), lambda j, i: (0, j)))
        out_shape.append(jax.ShapeDtypeStruct((mx, width), xdtype))
        est += 2 * (mx * d * 2 + mx * tn * jnp.dtype(xdtype).itemsize)
    in_specs.append(pl.BlockSpec((pl.Element(1), pl.Element(tn), pl.Element(d)),
                                 lambda j, i: (layer, pl.multiple_of(col0 + j * tn, SUBLANES), 0)))
    args.append(w_t)
    if bias is not None:
        in_specs.append(_lspec((1, tn), layer, lambda j, i: (0, j)))
        args.append(bias)
    outs = pl.pallas_call(
        functools.partial(_proj_body, act=act, sub=256, cast=cast, biased=bias is not None,
                          extra=extra is not None, n_main=n_main),
        grid=(width // tn, n_main + (extra is not None)),
        in_specs=in_specs,
        out_specs=out_specs,
        out_shape=out_shape,
        scratch_shapes=[pltpu.VMEM((tn, d), BF16)] if cast else [],
        compiler_params=pltpu.CompilerParams(
            dimension_semantics=("arbitrary", "arbitrary"), vmem_limit_bytes=_vmem_limit(est)),
        name="proj",
    )(*args)
    return outs if extra is not None else outs[0]


def _identity(t):
    return t


def _sgu_body(u_ref, v_ref, lng_ref, lnb_ref, mix_ref, bias_ref, wa_ref, *rest, emit_v, chained):
    if emit_v:
        if chained:
            rest = rest[1:]
        a_ref, vout_ref, ya_scr = rest
    else:
        a_ref, ya_scr = rest
    vg = v_ref[...].astype(F32)
    mu = jnp.mean(vg, axis=-1, keepdims=True)
    var = jnp.mean(jnp.square(vg - mu), axis=-1, keepdims=True)
    v = (vg - mu) * lax.rsqrt(var + EPS) * lng_ref[...] + lnb_ref[...]
    if emit_v:
        vout_ref[...] = v
    vb = v.astype(BF16)
    rows, width = vb.shape
    for c in range(rows // SGU_CHUNK):
        rs = slice(c * SGU_CHUNK, (c + 1) * SGU_CHUNK)
        for grp in range(width // SGU_GROUP_DIM):
            cs = slice(grp * SGU_GROUP_DIM, (grp + 1) * SGU_GROUP_DIM)
            mixed = _dot(mix_ref[grp], vb[rs, cs]) + bias_ref[:, cs]
            ya_scr[rs, cs] = (u_ref[rs, cs].astype(F32) * mixed).astype(BF16)
    a_ref[...] = _dot(ya_scr[...], wa_ref[...]).astype(a_ref.dtype)


def _sgu(su, lng, lnb, mix, bias, w_out_a, layer, rows, emit_v, prev=None):
    m = su.shape[0]
    width = su.shape[1] // 2
    depth, _, d = w_out_a.shape
    ngrp = mix.shape[1]
    isz = su.dtype.itemsize
    est = 2 * (2 * rows * width * isz + ngrp * SGU_CHUNK * SGU_CHUNK * 2 + SGU_CHUNK * width * 4
               + rows * d * 2 + rows * width * 4) + width * d * 2 + 4 * rows * width * 4
    out_shape = [jax.ShapeDtypeStruct((m, d), BF16)]
    out_specs = [pl.BlockSpec((rows, d), lambda i: (i, 0))]
    if emit_v:
        out_shape.append(jax.ShapeDtypeStruct((depth, m, width), F32))
        out_specs.append(pl.BlockSpec((None, rows, width), lambda i: (layer, i, 0)))
    in_specs = [
        pl.BlockSpec((rows, width), lambda i: (i, 0)),
        pl.BlockSpec((rows, width), lambda i: (i, 1)),
        _lspec((1, width), layer, lambda i: (0, 0)),
        _lspec((1, width), layer, lambda i: (0, 0)),
        _lspec((ngrp, SGU_CHUNK, SGU_CHUNK), layer, lambda i: (0, 0, 0)),
        _lspec((SGU_CHUNK, width), layer, lambda i: (0, 0)),
        pl.BlockSpec((None, width, d), lambda i: (layer, 0, 0), pipeline_mode=pl.Buffered(1)),
    ]
    args = [su, su, lng, lnb, mix, bias, w_out_a]
    aliases = _chain(args, in_specs, prev, 1)
    return pl.pallas_call(
        functools.partial(_sgu_body, emit_v=emit_v, chained=prev is not None),
        grid=(m // rows,),
        in_specs=in_specs,
        out_specs=out_specs,
        out_shape=out_shape,
        scratch_shapes=[pltpu.VMEM((rows, width), BF16)],
        input_output_aliases=aliases,
        compiler_params=pltpu.CompilerParams(
            dimension_semantics=("arbitrary",), vmem_limit_bytes=_vmem_limit(est)),
        name="sgu",
    )(*args)


def _prefix_sum_rows(x, row):
    n = x.shape[0]
    s = 1
    while s < n:
        x = x + jnp.where(row >= s, pltpu.roll(x, s, 0), 0.0)
        s *= 2
    return x


def _gate_groupnorm(y, z, ng):
    t = y * _silu(z)
    return t * lax.rsqrt(jnp.mean(t * t, axis=-1, keepdims=True) + EPS) * ng


def _split_hi_lo(v):
    hi = v.astype(BF16)
    lo = (v - hi.astype(F32)).astype(BF16)
    return jnp.concatenate([hi, lo], axis=1)


def _ssd_prompt_body(xbc_ref, z_ref, dt_ref, cw_ref, cb_ref, alog_ref, dexp_ref, ng_ref, e2_ref, shift_ref, *rest,
                     d_inner, n_chunks, chained):
    if chained:
        rest = rest[2:]
    yb_ref, h_ref, cst_ref, win_scr, act_scr, y_scr, ht_scr, cst_scr = rest
    c = pl.program_id(1)
    q = xbc_ref.shape[0]
    conv_dim = xbc_ref.shape[1]
    p = SSM_HEAD_DIM
    n_state = SSM_STATE
    gw = d_inner // SSM_GROUPS
    pairs_per_group = gw // (2 * p)
    halo = CONV_HALO

    @pl.when(c == 0)
    def _():
        win_scr[...] = jnp.zeros(win_scr.shape, BF16)
        ht_scr[...] = jnp.zeros(ht_scr.shape, F32)

    win_scr[halo:halo + q, :] = xbc_ref[...]
    lane_chunk = 512
    for j in range(conv_dim // lane_chunk):
        ls = slice(j * lane_chunk, (j + 1) * lane_chunk)
        shifted = _dot(shift_ref[...], win_scr[:, ls])
        acc = cb_ref[:, ls] + cw_ref[CONV_WIDTH - 1:CONV_WIDTH, ls] * xbc_ref[:, ls].astype(F32)
        for k in range(CONV_WIDTH - 1):
            acc = acc + cw_ref[k:k + 1, ls] * shifted[k * q:(k + 1) * q, :]
        act_scr[:, ls] = _silu(acc)

    @pl.when(c == n_chunks - 1)
    def _():
        tail = xbc_ref[q - halo:q, :].astype(F32)
        cst_ref[...] = tail[halo - (CONV_WIDTH - 1):halo, :]

    win_scr[0:halo, :] = xbc_ref[q - halo:q, :]

    row = lax.broadcasted_iota(jnp.int32, (q, LANES), 0)
    lane = lax.broadcasted_iota(jnp.int32, (q, LANES), 1)
    dt = dt_ref[...]
    da = dt * (-jnp.exp(alog_ref[...]))
    cs = _prefix_sum_rows(da, row)
    cs2 = cs * LOG2_E
    cst_scr[...] = cs2.T
    dt_hl = _split_hi_lo(dt)
    ecs_hl = _split_hi_lo(jnp.exp(cs))
    dec_hl = _split_hi_lo(jnp.exp(cs[q - 1:q, :] - cs))
    causal = row >= lane
    lower_half = lane < p

    for grp in range(SSM_GROUPS):
        gs = slice(grp * gw, (grp + 1) * gw)
        e2 = e2_ref[:, gs]
        dt_g = _dot(dt_hl, e2)
        ecs_g = _dot(ecs_hl, e2)
        dec_g = _dot(dec_hl, e2)
        xs_g = act_scr[:, gs]
        xdt_g = xs_g * dt_g
        b_g = act_scr[:, d_inner + grp * n_state:d_inner + (grp + 1) * n_state].astype(BF16)
        c_g = act_scr[:, d_inner + (SSM_GROUPS + grp) * n_state:
                      d_inner + (SSM_GROUPS + grp + 1) * n_state].astype(BF16)
        cbm = jnp.where(causal, _dot_nt(c_g, b_g), 0.0)
        ht_g = ht_scr[:, gs]
        y_g = _dot(c_g, ht_g.astype(BF16)) * ecs_g + dexp_ref[:, gs] * xs_g
        for j in range(pairs_per_group):
            h0 = grp * 2 * pairs_per_group + 2 * j
            ws = []
            for h in (h0, h0 + 1):
                seg2 = cs2[:, h:h + 1] - cst_scr[h:h + 1, :]
                ws.append((cbm * jnp.exp2(jnp.where(causal, seg2, 0.0))).astype(BF16))
            xpair = xdt_g[:, j * LANES:(j + 1) * LANES]
            rhs = jnp.concatenate([jnp.where(lower_half, xpair, 0.0).astype(BF16),
                                   jnp.where(lower_half, 0.0, xpair).astype(BF16)], axis=0)
            ls = slice(grp * gw + j * LANES, grp * gw + (j + 1) * LANES)
            y_scr[:, ls] = y_g[:, j * LANES:(j + 1) * LANES] + _dot(jnp.concatenate(ws, axis=1), rhs)
        states_t = _dot_tn(b_g, (xdt_g * dec_g).astype(BF16))
        ht_scr[:, gs] = ht_g * ecs_g[q - 1:q, :] + states_t

    for grp in range(SSM_GROUPS):
        gs = slice(grp * gw, (grp + 1) * gw)
        yb_ref[:, gs] = _gate_groupnorm(y_scr[:, gs], z_ref[:, gs].astype(F32), ng_ref[:, gs]).astype(yb_ref.dtype)

    @pl.when(c == n_chunks - 1)
    def _():
        for grp in range(SSM_GROUPS):
            gs = slice(grp * gw, (grp + 1) * gw)
            h_ref[gs, :] = ht_scr[:, gs].T


def _ssd_prompt(xbc, z, dt, conv_w, conv_b, a_log, d_exp, norm_g, e2, shift, batch, seq, depth, layer, prev):
    q = SSD_CHUNK
    n_chunks = seq // q
    conv_dim = xbc.shape[1]
    d_inner = z.shape[1]
    dtw = dt.shape[1]
    est = (2 * (q * conv_dim * 2 + q * d_inner * 2 + q * dtw * 4 + q * d_inner * 2 + d_inner * SSM_STATE * 4
                + 8 * conv_dim * 4 + 5 * conv_dim * 4 + 2 * d_inner * 4 + 2 * LANES * d_inner * 2)
           + (q + 16) * conv_dim * 4 + q * conv_dim * 4 + 2 * q * d_inner * 4 + (8 << 20))
    row_map = lambda b, c: (b * n_chunks + c, 0)
    const = lambda b, c: (0, 0)
    in_specs = [
        pl.BlockSpec((q, conv_dim), row_map),
        pl.BlockSpec((q, d_inner), row_map),
        pl.BlockSpec((q, dtw), row_map),
        _lspec((CONV_WIDTH, conv_dim), layer, const),
        _lspec((1, conv_dim), layer, const),
        _lspec((1, dtw), layer, const),
        _lspec((1, d_inner), layer, const),
        _lspec((1, d_inner), layer, const),
        pl.BlockSpec((2 * LANES, d_inner), const),
        pl.BlockSpec(shift.shape, const),
    ]
    args = [xbc, z, dt, conv_w, conv_b, a_log, d_exp, norm_g, e2, shift]
    aliases = _chain(args, in_specs, prev, 1)
    return pl.pallas_call(
        functools.partial(_ssd_prompt_body, d_inner=d_inner, n_chunks=n_chunks, chained=prev is not None),
        grid=(batch, n_chunks),
        in_specs=in_specs,
        out_specs=[
            pl.BlockSpec((q, d_inner), row_map),
            pl.BlockSpec((None, None, d_inner, SSM_STATE), lambda b, c: (layer, b, 0, 0)),
            pl.BlockSpec((None, None, CONV_WIDTH - 1, conv_dim), lambda b, c: (layer, b, 0, 0)),
        ],
        out_shape=[
            jax.ShapeDtypeStruct((batch * seq, d_inner), BF16),
            jax.ShapeDtypeStruct((depth, batch, d_inner, SSM_STATE), F32),
            jax.ShapeDtypeStruct((depth, batch, CONV_WIDTH - 1, conv_dim), F32),
        ],
        scratch_shapes=[
            pltpu.VMEM((2 * q, conv_dim), BF16),
            pltpu.VMEM((q, conv_dim), F32),
            pltpu.VMEM((q, d_inner), F32),
            pltpu.VMEM((SSM_STATE, d_inner), F32),
            pltpu.VMEM((LANES, q), F32),
        ],
        input_output_aliases=aliases,
        compiler_params=pltpu.CompilerParams(
            dimension_semantics=("arbitrary", "arbitrary"), vmem_limit_bytes=_vmem_limit(est)),
        name="ssd_prompt",
    )(*args)


def _ssd_sample_body(xbc_ref, z_ref, dte_ref, cbuf_ref, h0_ref, cw_ref, cb_ref, aexp_ref, dexp_ref, ng_ref,
                     *rest, d_inner, steps, chained):
    if chained:
        rest = rest[2:]
    yb_ref, h_ref, cst_ref, win_scr = rest
    q = xbc_ref.shape[0]
    nseq = q // steps
    conv_dim = xbc_ref.shape[1]
    n_state = SSM_STATE
    gw = d_inner // SSM_GROUPS
    halo = SUBLANES
    span = 2 * SUBLANES

    win_scr[...] = jnp.zeros(win_scr.shape, F32)
    for j in range(nseq):
        win_scr[j * span + halo - 3:j * span + halo, :] = cbuf_ref[j]
        win_scr[j * span + halo:j * span + halo + steps, :] = xbc_ref[j * steps:(j + 1) * steps, :]
        cst_ref[j] = win_scr[j * span + halo + steps - 3:j * span + halo + steps, :]
    rowc = lax.broadcasted_iota(jnp.int32, (q, conv_dim), 0)
    acc = None
    for j in range(nseq):
        base = j * span + halo - 3 - j * steps
        acc_j = cb_ref[...] + cw_ref[0:1, :] * win_scr[base:base + q, :]
        for k in range(1, CONV_WIDTH):
            acc_j = acc_j + cw_ref[k:k + 1, :] * win_scr[base + k:base + k + q, :]
        acc = acc_j if acc is None else jnp.where(rowc >= j * steps, acc_j, acc)
    act = _silu(acc)

    xs = act[:, :d_inner]
    row = lax.broadcasted_iota(jnp.int32, (q, d_inner), 0)
    tok = row % steps
    dt = dte_ref[...]
    da = dt * aexp_ref[...]
    cs = da
    s = 1
    while s < steps:
        cs = cs + jnp.where(tok >= s, pltpu.roll(cs, s, 0), 0.0)
        s *= 2
    cs_last = cs[steps - 1:steps, :]
    for j in range(1, nseq):
        cs_last = jnp.where(row >= j * steps, cs[(j + 1) * steps - 1:(j + 1) * steps, :], cs_last)
    xdt = xs * dt

    b_all = act[:, d_inner:d_inner + SSM_GROUPS * n_state]
    c_all = act[:, d_inner + SSM_GROUPS * n_state:]
    y = dexp_ref[...] * xs
    for o in range(steps):
        b_sh = b_all if o == 0 else pltpu.roll(b_all, o, 0)
        cs_sh = cs if o == 0 else pltpu.roll(cs, o, 0)
        xdt_sh = xdt if o == 0 else pltpu.roll(xdt, o, 0)
        prod = c_all * b_sh
        cb_o = jnp.concatenate(
            [jnp.broadcast_to(jnp.sum(prod[:, grp * n_state:(grp + 1) * n_state], axis=1, keepdims=True), (q, gw))
             for grp in range(SSM_GROUPS)], axis=1)
        keep = tok >= o
        lmat = jnp.where(keep, jnp.exp(jnp.where(keep, cs - cs_sh, 0.0)), 0.0)
        y = y + cb_o * lmat * xdt_sh

    ecs = jnp.exp(cs)
    xs_dec = xdt * jnp.exp(cs_last - cs)
    cd = jnp.exp(cs_last)
    rowg = lax.broadcasted_iota(jnp.int32, (q, gw), 0)
    lane = lax.broadcasted_iota(jnp.int32, (q, LANES), 1)
    p = SSM_HEAD_DIM
    for grp in range(SSM_GROUPS):
        gs = slice(grp * gw, (grp + 1) * gw)
        b_g = b_all[:, grp * n_state:(grp + 1) * n_state].astype(BF16)
        c_g = c_all[:, grp * n_state:(grp + 1) * n_state].astype(BF16)
        head_cd = []
        for t in range(gw // LANES):
            v = cd[:, grp * gw + t * LANES:grp * gw + (t + 1) * LANES]
            vr = pltpu.roll(v, p, 1)
            head_cd += [jnp.where(lane < p, v, vr), jnp.where(lane < p, vr, v)]
        y_off = None
        for j in range(nseq):
            h_g = h0_ref[j, gs, :]
            mine = (rowg >= j * steps) & (rowg < (j + 1) * steps)
            y_off_j = _dot_nt(c_g, h_g.astype(BF16))
            y_off = y_off_j if y_off is None else jnp.where(mine, y_off_j, y_off)
            states = _dot_tn(jnp.where(mine, xs_dec[:, gs], 0.0).astype(BF16), b_g)
            scale = jnp.concatenate(
                [jnp.broadcast_to(hc[j * steps:j * steps + 1, :], (p, n_state)) for hc in head_cd], axis=0)
            h_ref[j, gs, :] = h_g * scale + states
        y_g = y[:, gs] + y_off * ecs[:, gs]
        yb_ref[:, gs] = _gate_groupnorm(y_g, z_ref[:, gs].astype(F32), ng_ref[:, gs]).astype(yb_ref.dtype)


def _ssd_sample(xbc, z, dte, conv_state, h0_all, layer, conv_w, conv_b, a_exp, d_exp, norm_g, steps, prev):
    m, conv_dim = xbc.shape
    d_inner = z.shape[1]
    depth, nseq_total = h0_all.shape[:2]
    q = SUBLANES
    nseq = q // steps
    est = (2 * (2 * nseq * d_inner * SSM_STATE * 4 + q * (conv_dim + 3 * d_inner) * 4
                + 2 * nseq * SUBLANES * conv_dim * 4 + 5 * conv_dim * 4 + 3 * d_inner * 4)
           + 2 * nseq * SUBLANES * conv_dim * 4 + d_inner * LANES * 4 + (12 << 20))
    row_map = lambda s: (s, 0)
    lay_map = lambda s: (layer, s, 0, 0)
    const = lambda s: (0, 0)
    in_specs = [
        pl.BlockSpec((q, conv_dim), row_map),
        pl.BlockSpec((q, d_inner), row_map),
        pl.BlockSpec((q, d_inner), row_map),
        pl.BlockSpec((None, nseq, CONV_WIDTH - 1, conv_dim), lay_map),
        pl.BlockSpec((None, nseq, d_inner, SSM_STATE), lay_map),
        _lspec((CONV_WIDTH, conv_dim), layer, const),
        _lspec((1, conv_dim), layer, const),
        _lspec((1, d_inner), layer, const),
        _lspec((1, d_inner), layer, const),
        _lspec((1, d_inner), layer, const),
    ]
    args = [xbc, z, dte, conv_state, h0_all, conv_w, conv_b, a_exp, d_exp, norm_g]
    aliases = _chain(args, in_specs, prev, 1)
    return pl.pallas_call(
        functools.partial(_ssd_sample_body, d_inner=d_inner, steps=steps, chained=prev is not None),
        grid=(m // q,),
        in_specs=in_specs,
        out_specs=[
            pl.BlockSpec((q, d_inner), row_map),
            pl.BlockSpec((None, nseq, d_inner, SSM_STATE), lay_map),
            pl.BlockSpec((None, nseq, CONV_WIDTH - 1, conv_dim), lay_map),
        ],
        out_shape=[
            jax.ShapeDtypeStruct((m, d_inner), F32),
            jax.ShapeDtypeStruct((depth, nseq_total, d_inner, SSM_STATE), F32),
            jax.ShapeDtypeStruct((depth, nseq_total, CONV_WIDTH - 1, conv_dim), F32),
        ],
        scratch_shapes=[pltpu.VMEM((nseq * 2 * SUBLANES, conv_dim), F32)],
        input_output_aliases=aliases,
        compiler_params=pltpu.CompilerParams(
            dimension_semantics=("arbitrary",), vmem_limit_bytes=_vmem_limit(est)),
        name="ssd_sample",
    )(*args)


def _merge_body(a_ref, yb_ref, ga_ref, gb_ref, x_ref, wb_ref, wo_ref, o_ref):
    b = _dot(yb_ref[...].astype(BF16), wb_ref[...])
    merged = ga_ref[...].astype(F32) * a_ref[...].astype(F32) + gb_ref[...].astype(F32) * b
    o_ref[...] = x_ref[...] + _dot(merged.astype(BF16), wo_ref[...])


def _merge(a, yb, gates, x, w_out_b, w_o, layer, rows):
    m, d = x.shape
    d_inner = yb.shape[1]
    est = (2 * rows * (d * 2 + d_inner * yb.dtype.itemsize + 2 * d * gates.dtype.itemsize + 2 * d * 4 + d * 2)
           + d_inner * d * 2 + d * d * 2 + 3 * rows * d * 4)
    return pl.pallas_call(
        _merge_body,
        grid=(m // rows,),
        in_specs=[
            pl.BlockSpec((rows, d), lambda i: (i, 0)),
            pl.BlockSpec((rows, d_inner), lambda i: (i, 0)),
            pl.BlockSpec((rows, d), lambda i: (i, 0)),
            pl.BlockSpec((rows, d), lambda i: (i, 1)),
            pl.BlockSpec((rows, d), lambda i: (i, 0)),
            pl.BlockSpec((None, d_inner, d), lambda i: (layer, 0, 0), pipeline_mode=pl.Buffered(1)),
            pl.BlockSpec((None, d, d), lambda i: (layer, 0, 0), pipeline_mode=pl.Buffered(1)),
        ],
        out_specs=pl.BlockSpec((rows, d), lambda i: (i, 0)),
        out_shape=jax.ShapeDtypeStruct((m, d), F32),
        compiler_params=pltpu.CompilerParams(
            dimension_semantics=("arbitrary",), vmem_limit_bytes=_vmem_limit(est)),
        name="merge",
    )(a, yb, gates, gates, x, w_out_b, w_o)


def _ffn_body(x_ref, gin_ref, wg_ref, wu_ref, wd_ref, g_ref, o_ref, *rest, n_f, final, sub):
    h_scr = rest[-1]
    f = pl.program_id(1)
    blocks = _row_blocks(x_ref.shape[0], sub)

    @pl.when(f == 0)
    def _():
        for rs in blocks:
            x = x_ref[rs, :]
            h_scr[rs, :] = _rmsnorm(x, gin_ref[...]).astype(BF16)
            o_ref[rs, :] = x

    for rs in blocks:
        hb = h_scr[rs, :]
        act = _silu(_dot(hb, wg_ref[...])) * _dot(hb, wu_ref[...])
        o_ref[rs, :] += _dot(act.astype(BF16), wd_ref[...])

    @pl.when(f == n_f - 1)
    def _():
        for rs in blocks:
            y = _rmsnorm(o_ref[rs, :], g_ref[...])
            if final:
                o_ref[rs, :] = y
            else:
                rest[0][rs, :] = y.astype(BF16)


def _ffn(x, norm_g, w_gate, w_up, w_down, next_g, layer, g_layer, tm, tf, final):
    m, d = x.shape
    hidden = w_gate.shape[2]
    n_f = hidden // tf
    est = 2 * (tm * d * 4 + tm * d * 4 + tm * d * 2 + 3 * d * tf * 2) + tm * d * 2 + 256 * (2 * tf + d) * 4
    out_specs = [pl.BlockSpec((tm, d), lambda i, j: (i, 0))]
    out_shape = [jax.ShapeDtypeStruct((m, d), F32)]
    if not final:
        out_specs.append(pl.BlockSpec((tm, d), lambda i, j: (i, 0)))
        out_shape.append(jax.ShapeDtypeStruct((m, d), BF16))
    return pl.pallas_call(
        functools.partial(_ffn_body, n_f=n_f, final=final, sub=256),
        grid=(m // tm, n_f),
        in_specs=[
            pl.BlockSpec((tm, d), lambda i, j: (i, 0)),
            _lspec((1, d), layer, lambda i, j: (0, 0)),
            _lspec((d, tf), layer, lambda i, j: (0, j)),
            _lspec((d, tf), layer, lambda i, j: (0, j)),
            _lspec((tf, d), layer, lambda i, j: (j, 0)),
            _lspec((1, d), g_layer, lambda i, j: (0, 0)),
        ],
        out_specs=out_specs,
        out_shape=out_shape,
        scratch_shapes=[pltpu.VMEM((tm, d), BF16)],
        compiler_params=pltpu.CompilerParams(
            dimension_semantics=("arbitrary", "arbitrary"), vmem_limit_bytes=_vmem_limit(est)),
        name="ffn",
    )(x, norm_g, w_gate, w_up, w_down, next_g)


def _row_tile(m, target):
    t = min(m, target)
    while m % t:
        t //= 2
    return t


def kernel(x_prompt, x_sample, state_ssm, state_conv, norm1_g, w_in, conv_w, conv_b, dt_bias, a_log, d_skip, ssm_norm_g, sgu_ln_g, sgu_ln_b, sgu_w, sgu_b, w_out_a, w_out_b, w_o, norm2_g, w_ffn_gate, w_ffn_up, w_ffn_down, final_norm_g):
    batch, seq, d = x_prompt.shape
    nseq, steps, _ = x_sample.shape
    depth = w_in.shape[0]
    heads = dt_bias.shape[1]
    d_inner = heads * SSM_HEAD_DIM
    conv_dim = conv_w.shape[2]
    sgu_width = sgu_ln_g.shape[1]
    ngrp = sgu_w.shape[1]
    seg_widths = (2 * sgu_width, d_inner, conv_dim, 2 * d)
    dt_col = 2 * sgu_width + d_inner + conv_dim

    xp = x_prompt.reshape(batch * seq, d)
    xs = x_sample.reshape(nseq * steps, d)
    state_ssm_flat = state_ssm.reshape(depth, nseq, d_inner, SSM_STATE)

    def rows3(p):
        return p.reshape(depth, 1, p.shape[-1])

    w_in_t = jnp.swapaxes(w_in, 1, 2)
    dt_b = rows3(jnp.pad(dt_bias, ((0, 0), (0, LANES - heads))))
    w_dt_exp_t = jnp.repeat(w_in_t[:, dt_col:dt_col + heads, :], SSM_HEAD_DIM, axis=1).astype(BF16)
    dt_b_exp = rows3(jnp.repeat(dt_bias, SSM_HEAD_DIM, axis=1))
    a_log_pad = rows3(jnp.pad(a_log, ((0, 0), (0, LANES - heads))))
    a_exp = rows3(jnp.repeat(-jnp.exp(a_log), SSM_HEAD_DIM, axis=1))
    d_exp = rows3(jnp.repeat(d_skip, SSM_HEAD_DIM, axis=1))
    n1, n2, ng = rows3(norm1_g), rows3(norm2_g), rows3(ssm_norm_g)
    cb, lng, lnb = rows3(conv_b), rows3(sgu_ln_g), rows3(sgu_ln_b)
    tril = jnp.tril(jnp.ones((SGU_CHUNK, SGU_CHUNK), bool))
    mix_p = jnp.where(tril, sgu_w, 0).astype(BF16)
    bias_p = jnp.repeat(jnp.swapaxes(sgu_b, 1, 2), SGU_GROUP_DIM, axis=2)
    seqs_per_chunk = SGU_CHUNK // steps
    eye = jnp.eye(seqs_per_chunk, dtype=F32)
    w_small = jnp.where(jnp.tril(jnp.ones((steps, steps), bool)), sgu_w[:, :, :steps, :steps], 0)
    mix_s = jnp.einsum("ab,lgts->lgatbs", eye, w_small).reshape(depth, ngrp, SGU_CHUNK, SGU_CHUNK).astype(BF16)
    bias_s = jnp.tile(bias_p[:, :steps], (1, seqs_per_chunk, 1))
    wa, wb, wo = w_out_a.astype(BF16), w_out_b.astype(BF16), w_o.astype(BF16)
    wg, wu, wd = w_ffn_gate.astype(BF16), w_ffn_up.astype(BF16), w_ffn_down.astype(BF16)
    expand = (jnp.arange(d_inner)[None, :] // SSM_HEAD_DIM == jnp.arange(LANES)[:, None]).astype(BF16)
    e2 = jnp.concatenate([expand, expand], axis=0)
    rq = jnp.arange((CONV_WIDTH - 1) * SSD_CHUNK)
    shift = (jnp.arange(2 * SSD_CHUNK)[None, :]
             == (CONV_HALO - (CONV_WIDTH - 1) + rq % SSD_CHUNK + rq // SSD_CHUNK)[:, None]).astype(BF16)

    fin_g3 = final_norm_g.reshape(1, 1, d)
    w_su, w_z, w_xbc, w_gt = seg_widths

    def project(h, h_sample, l, tm):
        segs = ((0, w_su, jax.nn.gelu), (w_su, w_z, _identity), (w_su + w_z, w_xbc, _identity),
                (dt_col + heads, w_gt, jax.nn.sigmoid))
        outs = [_proj(h, w_in_t, l, c0, wdt, act, BF16, tm, 1024, extra=(h_sample, F32)) for c0, wdt, act in segs]
        return [o[0] for o in outs], [o[1] for o in outs]

    def finish(x, a, yb, gt, l, tm):
        x = _merge(a, yb, gt, x, wb, wo, l, _row_tile(x.shape[0], 256))
        if l == depth - 1:
            (y,) = _ffn(x, n2, wg, wu, wd, fin_g3, l, 0, tm, 512, True)
            return y, None
        return _ffn(x, n2, wg, wu, wd, n1, l, l + 1, tm, 512, False)

    tm_p = _row_tile(xp.shape[0], 1024)
    tm_s = _row_tile(xs.shape[0], 512)
    hp = _norm(xp, n1, 0, tm_p)
    hs = _norm(xs, n1, 0, tm_s)
    prev_p = prev_s = prev_v = None
    for l in range(depth):
        (su, z, xbc, gt), sample_proj = project(hp, hs, l, tm_p)

        dt = _proj(hp, w_in_t, l, dt_col, LANES, _softplus, F32, tm_p, LANES, bias=dt_b)
        (a,) = _sgu(su, lng, lnb, mix_p, bias_p, wa, l, _row_tile(xp.shape[0], 512), False)
        yb, *prev_p = _ssd_prompt(xbc, z, dt, conv_w, cb, a_log_pad, d_exp, ng, e2, shift, batch, seq, depth, l,
                                  prev_p)
        xp, hp = finish(xp, a, yb, gt, l, _row_tile(xp.shape[0], 512))

        su, z, xbc, gt = sample_proj
        dte = _proj(hs, w_dt_exp_t, l, 0, d_inner, _softplus, F32, tm_s, 1024, bias=dt_b_exp)
        a, *prev_v = _sgu(su, lng, lnb, mix_s, bias_s, wa, l, tm_s, True, prev_v)
        yb, *prev_s = _ssd_sample(xbc, z, dte, state_conv, state_ssm_flat, l, conv_w, cb, a_exp, d_exp, ng,
                                  steps, prev_s)
        xs, hs = finish(xs, a, yb, gt, l, tm_s)

    ssm_p, conv_p = prev_p
    ssm_s, conv_s = prev_s
    (v_s,) = prev_v
    return (xp.reshape(batch, seq, d), xs.reshape(nseq, steps, d),
            ssm_p.reshape(depth, batch, heads, SSM_HEAD_DIM, SSM_STATE), conv_p,
            ssm_s.reshape(depth, nseq, heads, SSM_HEAD_DIM, SSM_STATE), conv_s,
            v_s.reshape(depth, nseq, steps, sgu_width))
```

```python
import functools

import jax
import jax.numpy as jnp
from jax import lax
from jax.experimental import pallas as pl
from jax.experimental.pallas import tpu as pltpu

F32 = jnp.float32
BF16 = jnp.bfloat16
EPS = 1e-6
LOG2_E = 1.4426950408889634

LANES = 128
SUBLANES = 8
VMEM_CAP_BYTES = 64 * 1024 * 1024

SGU_CHUNK = 128
SGU_GROUP_DIM = 128
SSM_HEAD_DIM = 64
SSM_STATE = 128
SSM_GROUPS = 8
CONV_WIDTH = 4
CONV_HALO = 16
SSD_CHUNK = 128


def _vmem_limit(estimate_bytes):
    return int(min(estimate_bytes + (8 << 20), VMEM_CAP_BYTES - (6 << 20)))


def _silu(x):
    hx = 0.5 * x
    return hx + hx * jnp.tanh(hx)


def _softplus(x):
    return jnp.maximum(x, 0.0) + jnp.log1p(jnp.exp(-jnp.abs(x)))


def _rmsnorm(x, g):
    return x * lax.rsqrt(jnp.mean(x * x, axis=-1, keepdims=True) + EPS) * g


def _dot(a, b):
    return jnp.dot(a, b, preferred_element_type=F32)


def _dot_nt(a, b):
    return lax.dot_general(a, b, (((1,), (1,)), ((), ())), preferred_element_type=F32)


def _dot_tn(a, b):
    return lax.dot_general(a, b, (((0,), (0,)), ((), ())), preferred_element_type=F32)


def _chain(args, in_specs, prev, first_out):
    aliases = {}
    if prev is not None:
        for k, arr in enumerate(prev):
            aliases[len(args)] = first_out + k
            args.append(arr)
            in_specs.append(pl.BlockSpec(memory_space=pl.ANY))
    return aliases


def _row_blocks(rows, sub):
    sub = min(rows, sub)
    return [slice(r * sub, (r + 1) * sub) for r in range(rows // sub)]


def _lspec(block, layer, tail_index):
    return pl.BlockSpec((None,) + tuple(block), lambda *ids: (layer,) + tuple(tail_index(*ids)))


def _norm_body(x_ref, g_ref, o_ref):
    o_ref[...] = _rmsnorm(x_ref[...], g_ref[...]).astype(o_ref.dtype)


def _norm(x, g, layer, rows):
    m, d = x.shape
    est = 2 * rows * d * 6 + rows * d * 8
    return pl.pallas_call(
        _norm_body,
        grid=(m // rows,),
        in_specs=[pl.BlockSpec((rows, d), lambda i: (i, 0)), _lspec((1, d), layer, lambda i: (0, 0))],
        out_specs=pl.BlockSpec((rows, d), lambda i: (i, 0)),
        out_shape=jax.ShapeDtypeStruct((m, d), BF16),
        compiler_params=pltpu.CompilerParams(
            dimension_semantics=("arbitrary",), vmem_limit_bytes=_vmem_limit(est)),
        name="norm",
    )(x, g)


def _proj_body(*refs, act, sub, cast, biased, extra, n_main):
    refs = list(refs)
    h_ref = refs.pop(0)
    hx_ref = refs.pop(0) if extra else None
    w_ref = refs.pop(0)
    b_ref = refs.pop(0) if biased else None
    o_ref = refs.pop(0)
    ox_ref = refs.pop(0) if extra else None
    i = pl.program_id(1)
    if cast:
        (w_bf,) = refs

        @pl.when(i == 0)
        def _():
            w_bf[...] = w_ref[0].astype(BF16)
    else:
        w_bf = w_ref.at[0]

    def run(src_ref, dst_ref):
        def body():
            for rs in _row_blocks(src_ref.shape[0], sub):
                t = _dot_nt(src_ref[rs, :], w_bf[...])
                if biased:
                    t = t + b_ref[...]
                dst_ref[rs, :] = act(t).astype(dst_ref.dtype)
        return body

    if extra:
        pl.when(i < n_main)(run(h_ref, o_ref))
        pl.when(i == n_main)(run(hx_ref, ox_ref))
    else:
        run(h_ref, o_ref)()


def _proj(h, w_t, layer, col0, width, act, out_dtype, tm, tn, bias=None, extra=None):
    m, d = h.shape
    cast = w_t.dtype != BF16
    n_main = m // tm
    osz = jnp.dtype(out_dtype).itemsize
    est = 2 * (tm * d * 2 + d * tn * w_t.dtype.itemsize + tm * tn * osz + tn * 4) + d * tn * 2 + 256 * tn * 8
    last = n_main - 1
    in_specs = [pl.BlockSpec((tm, d), lambda j, i: (jnp.minimum(i, last), 0))]
    args = [h]
    out_specs = [pl.BlockSpec((tm, tn), lambda j, i: (jnp.minimum(i, last), j))]
    out_shape = [jax.ShapeDtypeStruct((m, width), out_dtype)]
    if extra is not None:
        hx, xdtype = extra
        mx = hx.shape[0]
        in_specs.append(pl.BlockSpec((mx, d), lambda j, i: (0, 0)))
        args.append(hx)
        out_specs.append(pl.BlockSpec((mx, tn), lambda j, i: (0, j)))
        out_shape.append(jax.ShapeDtypeStruct((mx, width), xdtype))
        est += 2 * (mx * d * 2 + mx * tn * jnp.dtype(xdtype).itemsize)
    in_specs.append(pl.BlockSpec((pl.Element(1), pl.Element(tn), pl.Element(d)),
                                 lambda j, i: (layer, pl.multiple_of(col0 + j * tn, SUBLANES), 0)))
    args.append(w_t)
    if bias is not None:
        in_specs.append(_lspec((1, tn), layer, lambda j, i: (0, j)))
        args.append(bias)
    outs = pl.pallas_call(
        functools.partial(_proj_body, act=act, sub=256, cast=cast, biased=bias is not None,
                          extra=extra is not None, n_main=n_main),
        grid=(width // tn, n_main + (extra is not None)),
        in_specs=in_specs,
        out_specs=out_specs,
        out_shape=out_shape,
        scratch_shapes=[pltpu.VMEM((tn, d), BF16)] if cast else [],
        compiler_params=pltpu.CompilerParams(
            dimension_semantics=("arbitrary", "arbitrary"), vmem_limit_bytes=_vmem_limit(est)),
        name="proj",
    )(*args)
    return outs if extra is not None else outs[0]


def _identity(t):
    return t


def _sgu_body(u_ref, v_ref, lng_ref, lnb_ref, mix_ref, bias_ref, wa_ref, *rest, emit_v, chained):
    if emit_v:
        if chained:
            rest = rest[1:]
        a_ref, vout_ref, ya_scr = rest
    else:
        a_ref, ya_scr = rest
    vg = v_ref[...].astype(F32)
    mu = jnp.mean(vg, axis=-1, keepdims=True)
    var = jnp.mean(jnp.square(vg - mu), axis=-1, keepdims=True)
    v = (vg - mu) * lax.rsqrt(var + EPS) * lng_ref[...] + lnb_ref[...]
    if emit_v:
        vout_ref[...] = v
    vb = v.astype(BF16)
    rows, width = vb.shape
    for c in range(rows // SGU_CHUNK):
        rs = slice(c * SGU_CHUNK, (c + 1) * SGU_CHUNK)
        for grp in range(width // SGU_GROUP_DIM):
            cs = slice(grp * SGU_GROUP_DIM, (grp + 1) * SGU_GROUP_DIM)
            mixed = _dot(mix_ref[grp], vb[rs, cs]) + bias_ref[:, cs]
            ya_scr[rs, cs] = (u_ref[rs, cs].astype(F32) * mixed).astype(BF16)
    a_ref[...] = _dot(ya_scr[...], wa_ref[...]).astype(a_ref.dtype)


def _sgu(su, lng, lnb, mix, bias, w_out_a, layer, rows, emit_v, prev=None):
    m = su.shape[0]
    width = su.shape[1] // 2
    depth, _, d = w_out_a.shape
    ngrp = mix.shape[1]
    isz = su.dtype.itemsize
    est = 2 * (2 * rows * width * isz + ngrp * SGU_CHUNK * SGU_CHUNK * 2 + SGU_CHUNK * width * 4
               + rows * d * 2 + rows * width * 4) + width * d * 2 + 4 * rows * width * 4
    out_shape = [jax.ShapeDtypeStruct((m, d), BF16)]
    out_specs = [pl.BlockSpec((rows, d), lambda i: (i, 0))]
    if emit_v:
        out_shape.append(jax.ShapeDtypeStruct((depth, m, width), F32))
        out_specs.append(pl.BlockSpec((None, rows, width), lambda i: (layer, i, 0)))
    in_specs = [
        pl.BlockSpec((rows, width), lambda i: (i, 0)),
        pl.BlockSpec((rows, width), lambda i: (i, 1)),
        _lspec((1, width), layer, lambda i: (0, 0)),
        _lspec((1, width), layer, lambda i: (0, 0)),
        _lspec((ngrp, SGU_CHUNK, SGU_CHUNK), layer, lambda i: (0, 0, 0)),
        _lspec((SGU_CHUNK, width), layer, lambda i: (0, 0)),
        pl.BlockSpec((None, width, d), lambda i: (layer, 0, 0), pipeline_mode=pl.Buffered(1)),
    ]
    args = [su, su, lng, lnb, mix, bias, w_out_a]
    aliases = _chain(args, in_specs, prev, 1)
    return pl.pallas_call(
        functools.partial(_sgu_body, emit_v=emit_v, chained=prev is not None),
        grid=(m // rows,),
        in_specs=in_specs,
        out_specs=out_specs,
        out_shape=out_shape,
        scratch_shapes=[pltpu.VMEM((rows, width), BF16)],
        input_output_aliases=aliases,
        compiler_params=pltpu.CompilerParams(
            dimension_semantics=("arbitrary",), vmem_limit_bytes=_vmem_limit(est)),
        name="sgu",
    )(*args)


def _prefix_sum_rows(x, row):
    n = x.shape[0]
    s = 1
    while s < n:
        x = x + jnp.where(row >= s, pltpu.roll(x, s, 0), 0.0)
        s *= 2
    return x


def _gate_groupnorm(y, z, ng):
    t = y * _silu(z)
    return t * lax.rsqrt(jnp.mean(t * t, axis=-1, keepdims=True) + EPS) * ng


def _split_hi_lo(v):
    hi = v.astype(BF16)
    lo = (v - hi.astype(F32)).astype(BF16)
    return jnp.concatenate([hi, lo], axis=1)


def _ssd_prompt_body(xbc_ref, z_ref, dt_ref, cw_ref, cb_ref, alog_ref, dexp_ref, ng_ref, e2_ref, shift_ref, *rest,
                     d_inner, n_chunks, chained):
    if chained:
        rest = rest[2:]
    yb_ref, h_ref, cst_ref, win_scr, act_scr, y_scr, ht_scr, cst_scr = rest
    c = pl.program_id(1)
    q = xbc_ref.shape[0]
    conv_dim = xbc_ref.shape[1]
    p = SSM_HEAD_DIM
    n_state = SSM_STATE
    gw = d_inner // SSM_GROUPS
    pairs_per_group = gw // (2 * p)
    halo = CONV_HALO

    @pl.when(c == 0)
    def _():
        win_scr[...] = jnp.zeros(win_scr.shape, BF16)
        ht_scr[...] = jnp.zeros(ht_scr.shape, F32)

    win_scr[halo:halo + q, :] = xbc_ref[...]
    lane_chunk = 512
    for j in range(conv_dim // lane_chunk):
        ls = slice(j * lane_chunk, (j + 1) * lane_chunk)
        shifted = _dot(shift_ref[...], win_scr[:, ls])
        acc = cb_ref[:, ls] + cw_ref[CONV_WIDTH - 1:CONV_WIDTH, ls] * xbc_ref[:, ls].astype(F32)
        for k in range(CONV_WIDTH - 1):
            acc = acc + cw_ref[k:k + 1, ls] * shifted[k * q:(k + 1) * q, :]
        act_scr[:, ls] = _silu(acc)

    @pl.when(c == n_chunks - 1)
    def _():
        tail = xbc_ref[q - halo:q, :].astype(F32)
        cst_ref[...] = tail[halo - (CONV_WIDTH - 1):halo, :]

    win_scr[0:halo, :] = xbc_ref[q - halo:q, :]

    row = lax.broadcasted_iota(jnp.int32, (q, LANES), 0)
    lane = lax.broadcasted_iota(jnp.int32, (q, LANES), 1)
    dt = dt_ref[...]
    da = dt * (-jnp.exp(alog_ref[...]))
    cs = _prefix_sum_rows(da, row)
    cs2 = cs * LOG2_E
    cst_scr[...] = cs2.T
    dt_hl = _split_hi_lo(dt)
    ecs_hl = _split_hi_lo(jnp.exp(cs))
    dec_hl = _split_hi_lo(jnp.exp(cs[q - 1:q, :] - cs))
    causal = row >= lane
    lower_half = lane < p

    for grp in range(SSM_GROUPS):
        gs = slice(grp * gw, (grp + 1) * gw)
        e2 = e2_ref[:, gs]
        dt_g = _dot(dt_hl, e2)
        ecs_g = _dot(ecs_hl, e2)
        dec_g = _dot(dec_hl, e2)
        xs_g = act_scr[:, gs]
        xdt_g = xs_g * dt_g
        b_g = act_scr[:, d_inner + grp * n_state:d_inner + (grp + 1) * n_state].astype(BF16)
        c_g = act_scr[:, d_inner + (SSM_GROUPS + grp) * n_state:
                      d_inner + (SSM_GROUPS + grp + 1) * n_state].astype(BF16)
        cbm = jnp.where(causal, _dot_nt(c_g, b_g), 0.0)
        ht_g = ht_scr[:, gs]
        y_g = _dot(c_g, ht_g.astype(BF16)) * ecs_g + dexp_ref[:, gs] * xs_g
        for j in range(pairs_per_group):
            h0 = grp * 2 * pairs_per_group + 2 * j
            ws = []
            for h in (h0, h0 + 1):
                seg2 = cs2[:, h:h + 1] - cst_scr[h:h + 1, :]
                ws.append((cbm * jnp.exp2(jnp.where(causal, seg2, 0.0))).astype(BF16))
            xpair = xdt_g[:, j * LANES:(j + 1) * LANES]
            rhs = jnp.concatenate([jnp.where(lower_half, xpair, 0.0).astype(BF16),
                                   jnp.where(lower_half, 0.0, xpair).astype(BF16)], axis=0)
            ls = slice(grp * gw + j * LANES, grp * gw + (j + 1) * LANES)
            y_scr[:, ls] = y_g[:, j * LANES:(j + 1) * LANES] + _dot(jnp.concatenate(ws, axis=1), rhs)
        states_t = _dot_tn(b_g, (xdt_g * dec_g).astype(BF16))
        ht_scr[:, gs] = ht_g * ecs_g[q - 1:q, :] + states_t

    for grp in range(SSM_GROUPS):
        gs = slice(grp * gw, (grp + 1) * gw)
        yb_ref[:, gs] = _gate_groupnorm(y_scr[:, gs], z_ref[:, gs].astype(F32), ng_ref[:, gs]).astype(yb_ref.dtype)

    @pl.when(c == n_chunks - 1)
    def _():
        for grp in range(SSM_GROUPS):
            gs = slice(grp * gw, (grp + 1) * gw)
            h_ref[gs, :] = ht_scr[:, gs].T


def _ssd_prompt(xbc, z, dt, conv_w, conv_b, a_log, d_exp, norm_g, e2, shift, batch, seq, depth, layer, prev):
    q = SSD_CHUNK
    n_chunks = seq // q
    conv_dim = xbc.shape[1]
    d_inner = z.shape[1]
    dtw = dt.shape[1]
    est = (2 * (q * conv_dim * 2 + q * d_inner * 2 + q * dtw * 4 + q * d_inner * 2 + d_inner * SSM_STATE * 4
                + 8 * conv_dim * 4 + 5 * conv_dim * 4 + 2 * d_inner * 4 + 2 * LANES * d_inner * 2)
           + (q + 16) * conv_dim * 4 + q * conv_dim * 4 + 2 * q * d_inner * 4 + (8 << 20))
    row_map = lambda b, c: (b * n_chunks + c, 0)
    const = lambda b, c: (0, 0)
    in_specs = [
        pl.BlockSpec((q, conv_dim), row_map),
        pl.BlockSpec((q, d_inner), row_map),
        pl.BlockSpec((q, dtw), row_map),
        _lspec((CONV_WIDTH, conv_dim), layer, const),
        _lspec((1, conv_dim), layer, const),
        _lspec((1, dtw), layer, const),
        _lspec((1, d_inner), layer, const),
        _lspec((1, d_inner), layer, const),
        pl.BlockSpec((2 * LANES, d_inner), const),
        pl.BlockSpec(shift.shape, const),
    ]
    args = [xbc, z, dt, conv_w, conv_b, a_log, d_exp, norm_g, e2, shift]
    aliases = _chain(args, in_specs, prev, 1)
    return pl.pallas_call(
        functools.partial(_ssd_prompt_body, d_inner=d_inner, n_chunks=n_chunks, chained=prev is not None),
        grid=(batch, n_chunks),
        in_specs=in_specs,
        out_specs=[
            pl.BlockSpec((q, d_inner), row_map),
            pl.BlockSpec((None, None, d_inner, SSM_STATE), lambda b, c: (layer, b, 0, 0)),
            pl.BlockSpec((None, None, CONV_WIDTH - 1, conv_dim), lambda b, c: (layer, b, 0, 0)),
        ],
        out_shape=[
            jax.ShapeDtypeStruct((batch * seq, d_inner), BF16),
            jax.ShapeDtypeStruct((depth, batch, d_inner, SSM_STATE), F32),
            jax.ShapeDtypeStruct((depth, batch, CONV_WIDTH - 1, conv_dim), F32),
        ],
        scratch_shapes=[
            pltpu.VMEM((2 * q, conv_dim), BF16),
            pltpu.VMEM((q, conv_dim), F32),
            pltpu.VMEM((q, d_inner), F32),
            pltpu.VMEM((SSM_STATE, d_inner), F32),
            pltpu.VMEM((LANES, q), F32),
        ],
        input_output_aliases=aliases,
        compiler_params=pltpu.CompilerParams(
            dimension_semantics=("arbitrary", "arbitrary"), vmem_limit_bytes=_vmem_limit(est)),
        name="ssd_prompt",
    )(*args)


def _ssd_sample_body(xbc_ref, z_ref, dte_ref, cbuf_ref, h0_ref, cw_ref, cb_ref, aexp_ref, dexp_ref, ng_ref,
                     *rest, d_inner, steps, chained):
    if chained:
        rest = rest[2:]
    yb_ref, h_ref, cst_ref, win_scr = rest
    q = xbc_ref.shape[0]
    nseq = q // steps
    conv_dim = xbc_ref.shape[1]
    n_state = SSM_STATE
    gw = d_inner // SSM_GROUPS
    halo = SUBLANES
    span = 2 * SUBLANES

    win_scr[...] = jnp.zeros(win_scr.shape, F32)
    for j in range(nseq):
        win_scr[j * span + halo - 3:j * span + halo, :] = cbuf_ref[j]
        win_scr[j * span + halo:j * span + halo + steps, :] = xbc_ref[j * steps:(j + 1) * steps, :]
        cst_ref[j] = win_scr[j * span + halo + steps - 3:j * span + halo + steps, :]
    rowc = lax.broadcasted_iota(jnp.int32, (q, conv_dim), 0)
    acc = None
    for j in range(nseq):
        base = j * span + halo - 3 - j * steps
        acc_j = cb_ref[...] + cw_ref[0:1, :] * win_scr[base:base + q, :]
        for k in range(1, CONV_WIDTH):
            acc_j = acc_j + cw_ref[k:k + 1, :] * win_scr[base + k:base + k + q, :]
        acc = acc_j if acc is None else jnp.where(rowc >= j * steps, acc_j, acc)
    act = _silu(acc)

    xs = act[:, :d_inner]
    row = lax.broadcasted_iota(jnp.int32, (q, d_inner), 0)
    tok = row % steps
    dt = dte_ref[...]
    da = dt * aexp_ref[...]
    cs = da
    s = 1
    while s < steps:
        cs = cs + jnp.where(tok >= s, pltpu.roll(cs, s, 0), 0.0)
        s *= 2
    cs_last = cs[steps - 1:steps, :]
    for j in range(1, nseq):
        cs_last = jnp.where(row >= j * steps, cs[(j + 1) * steps - 1:(j + 1) * steps, :], cs_last)
    xdt = xs * dt

    b_all = act[:, d_inner:d_inner + SSM_GROUPS * n_state]
    c_all = act[:, d_inner + SSM_GROUPS * n_state:]
    y = dexp_ref[...] * xs
    for o in range(steps):
        b_sh = b_all if o == 0 else pltpu.roll(b_all, o, 0)
        cs_sh = cs if o == 0 else pltpu.roll(cs, o, 0)
        xdt_sh = xdt if o == 0 else pltpu.roll(xdt, o, 0)
        prod = c_all * b_sh
        cb_o = jnp.concatenate(
            [jnp.broadcast_to(jnp.sum(prod[:, grp * n_state:(grp + 1) * n_state], axis=1, keepdims=True), (q, gw))
             for grp in range(SSM_GROUPS)], axis=1)
        keep = tok >= o
        lmat = jnp.where(keep, jnp.exp(jnp.where(keep, cs - cs_sh, 0.0)), 0.0)
        y = y + cb_o * lmat * xdt_sh

    ecs = jnp.exp(cs)
    xs_dec = xdt * jnp.exp(cs_last - cs)
    cd = jnp.exp(cs_last)
    rowg = lax.broadcasted_iota(jnp.int32, (q, gw), 0)
    lane = lax.broadcasted_iota(jnp.int32, (q, LANES), 1)
    p = SSM_HEAD_DIM
    for grp in range(SSM_GROUPS):
        gs = slice(grp * gw, (grp + 1) * gw)
        b_g = b_all[:, grp * n_state:(grp + 1) * n_state].astype(BF16)
        c_g = c_all[:, grp * n_state:(grp + 1) * n_state].astype(BF16)
        head_cd = []
        for t in range(gw // LANES):
            v = cd[:, grp * gw + t * LANES:grp * gw + (t + 1) * LANES]
            vr = pltpu.roll(v, p, 1)
            head_cd += [jnp.where(lane < p, v, vr), jnp.where(lane < p, vr, v)]
        y_off = None
        for j in range(nseq):
            h_g = h0_ref[j, gs, :]
            mine = (rowg >= j * steps) & (rowg < (j + 1) * steps)
            y_off_j = _dot_nt(c_g, h_g.astype(BF16))
            y_off = y_off_j if y_off is None else jnp.where(mine, y_off_j, y_off)
            states = _dot_tn(jnp.where(mine, xs_dec[:, gs], 0.0).astype(BF16), b_g)
            scale = jnp.concatenate(
                [jnp.broadcast_to(hc[j * steps:j * steps + 1, :], (p, n_state)) for hc in head_cd], axis=0)
            h_ref[j, gs, :] = h_g * scale + states
        y_g = y[:, gs] + y_off * ecs[:, gs]
        yb_ref[:, gs] = _gate_groupnorm(y_g, z_ref[:, gs].astype(F32), ng_ref[:, gs]).astype(yb_ref.dtype)


def _ssd_sample(xbc, z, dte, conv_state, h0_all, layer, conv_w, conv_b, a_exp, d_exp, norm_g, steps, prev):
    m, conv_dim = xbc.shape
    d_inner = z.shape[1]
    depth, nseq_total = h0_all.shape[:2]
    q = SUBLANES
    nseq = q // steps
    est = (2 * (2 * nseq * d_inner * SSM_STATE * 4 + q * (conv_dim + 3 * d_inner) * 4
                + 2 * nseq * SUBLANES * conv_dim * 4 + 5 * conv_dim * 4 + 3 * d_inner * 4)
           + 2 * nseq * SUBLANES * conv_dim * 4 + d_inner * LANES * 4 + (12 << 20))
    row_map = lambda s: (s, 0)
    lay_map = lambda s: (layer, s, 0, 0)
    const = lambda s: (0, 0)
    in_specs = [
        pl.BlockSpec((q, conv_dim), row_map),
        pl.BlockSpec((q, d_inner), row_map),
        pl.BlockSpec((q, d_inner), row_map),
        pl.BlockSpec((None, nseq, CONV_WIDTH - 1, conv_dim), lay_map),
        pl.BlockSpec((None, nseq, d_inner, SSM_STATE), lay_map),
        _lspec((CONV_WIDTH, conv_dim), layer, const),
        _lspec((1, conv_dim), layer, const),
        _lspec((1, d_inner), layer, const),
        _lspec((1, d_inner), layer, const),
        _lspec((1, d_inner), layer, const),
    ]
    args = [xbc, z, dte, conv_state, h0_all, conv_w, conv_b, a_exp, d_exp, norm_g]
    aliases = _chain(args, in_specs, prev, 1)
    return pl.pallas_call(
        functools.partial(_ssd_sample_body, d_inner=d_inner, steps=steps, chained=prev is not None),
        grid=(m // q,),
        in_specs=in_specs,
        out_specs=[
            pl.BlockSpec((q, d_inner), row_map),
            pl.BlockSpec((None, nseq, d_inner, SSM_STATE), lay_map),
            pl.BlockSpec((None, nseq, CONV_WIDTH - 1, conv_dim), lay_map),
        ],
        out_shape=[
            jax.ShapeDtypeStruct((m, d_inner), F32),
            jax.ShapeDtypeStruct((depth, nseq_total, d_inner, SSM_STATE), F32),
            jax.ShapeDtypeStruct((depth, nseq_total, CONV_WIDTH - 1, conv_dim), F32),
        ],
        scratch_shapes=[pltpu.VMEM((nseq * 2 * SUBLANES, conv_dim), F32)],
        input_output_aliases=aliases,
        compiler_params=pltpu.CompilerParams(
            dimension_semantics=("arbitrary",), vmem_limit_bytes=_vmem_limit(est)),
        name="ssd_sample",
    )(*args)


def _merge_body(a_ref, yb_ref, ga_ref, gb_ref, x_ref, wb_ref, wo_ref, o_ref):
    b = _dot(yb_ref[...].astype(BF16), wb_ref[...])
    merged = ga_ref[...].astype(F32) * a_ref[...].astype(F32) + gb_ref[...].astype(F32) * b
    o_ref[...] = x_ref[...] + _dot(merged.astype(BF16), wo_ref[...])


def _merge(a, yb, gates, x, w_out_b, w_o, layer, rows):
    m, d = x.shape
    d_inner = yb.shape[1]
    est = (2 * rows * (d * 2 + d_inner * yb.dtype.itemsize + 2 * d * gates.dtype.itemsize + 2 * d * 4 + d * 2)
           + d_inner * d * 2 + d * d * 2 + 3 * rows * d * 4)
    return pl.pallas_call(
        _merge_body,
        grid=(m // rows,),
        in_specs=[
            pl.BlockSpec((rows, d), lambda i: (i, 0)),
            pl.BlockSpec((rows, d_inner), lambda i: (i, 0)),
            pl.BlockSpec((rows, d), lambda i: (i, 0)),
            pl.BlockSpec((rows, d), lambda i: (i, 1)),
            pl.BlockSpec((rows, d), lambda i: (i, 0)),
            pl.BlockSpec((None, d_inner, d), lambda i: (layer, 0, 0), pipeline_mode=pl.Buffered(1)),
            pl.BlockSpec((None, d, d), lambda i: (layer, 0, 0), pipeline_mode=pl.Buffered(1)),
        ],
        out_specs=pl.BlockSpec((rows, d), lambda i: (i, 0)),
        out_shape=jax.ShapeDtypeStruct((m, d), F32),
        compiler_params=pltpu.CompilerParams(
            dimension_semantics=("arbitrary",), vmem_limit_bytes=_vmem_limit(est)),
        name="merge",
    )(a, yb, gates, gates, x, w_out_b, w_o)


def _ffn_body(x_ref, gin_ref, wg_ref, wu_ref, wd_ref, g_ref, o_ref, *rest, n_f, final, sub):
    h_scr = rest[-1]
    f = pl.program_id(1)
    blocks = _row_blocks(x_ref.shape[0], sub)

    @pl.when(f == 0)
    def _():
        for rs in blocks:
            x = x_ref[rs, :]
            h_scr[rs, :] = _rmsnorm(x, gin_ref[...]).astype(BF16)
            o_ref[rs, :] = x

    for rs in blocks:
        hb = h_scr[rs, :]
        act = _silu(_dot(hb, wg_ref[...])) * _dot(hb, wu_ref[...])
        o_ref[rs, :] += _dot(act.astype(BF16), wd_ref[...])

    @pl.when(f == n_f - 1)
    def _():
        for rs in blocks:
            y = _rmsnorm(o_ref[rs, :], g_ref[...])
            if final:
                o_ref[rs, :] = y
            else:
                rest[0][rs, :] = y.astype(BF16)


def _ffn(x, norm_g, w_gate, w_up, w_down, next_g, layer, g_layer, tm, tf, final):
    m, d = x.shape
    hidden = w_gate.shape[2]
    n_f = hidden // tf
    est = 2 * (tm * d * 4 + tm * d * 4 + tm * d * 2 + 3 * d * tf * 2) + tm * d * 2 + 256 * (2 * tf + d) * 4
    out_specs = [pl.BlockSpec((tm, d), lambda i, j: (i, 0))]
    out_shape = [jax.ShapeDtypeStruct((m, d), F32)]
    if not final:
        out_specs.append(pl.BlockSpec((tm, d), lambda i, j: (i, 0)))
        out_shape.append(jax.ShapeDtypeStruct((m, d), BF16))
    return pl.pallas_call(
        functools.partial(_ffn_body, n_f=n_f, final=final, sub=256),
        grid=(m // tm, n_f),
        in_specs=[
            pl.BlockSpec((tm, d), lambda i, j: (i, 0)),
            _lspec((1, d), layer, lambda i, j: (0, 0)),
            _lspec((d, tf), layer, lambda i, j: (0, j)),
            _lspec((d, tf), layer, lambda i, j: (0, j)),
            _lspec((tf, d), layer, lambda i, j: (j, 0)),
            _lspec((1, d), g_layer, lambda i, j: (0, 0)),
        ],
        out_specs=out_specs,
        out_shape=out_shape,
        scratch_shapes=[pltpu.VMEM((tm, d), BF16)],
        compiler_params=pltpu.CompilerParams(
            dimension_semantics=("arbitrary", "arbitrary"), vmem_limit_bytes=_vmem_limit(est)),
        name="ffn",
    )(x, norm_g, w_gate, w_up, w_down, next_g)


def _row_tile(m, target):
    t = min(m, target)
    while m % t:
        t //= 2
    return t


def kernel(x_prompt, x_sample, state_ssm, state_conv, norm1_g, w_in, conv_w, conv_b, dt_bias, a_log, d_skip, ssm_norm_g, sgu_ln_g, sgu_ln_b, sgu_w, sgu_b, w_out_a, w_out_b, w_o, norm2_g, w_ffn_gate, w_ffn_up, w_ffn_down, final_norm_g):
    batch, seq, d = x_prompt.shape
    nseq, steps, _ = x_sample.shape
    depth = w_in.shape[0]
    heads = dt_bias.shape[1]
    d_inner = heads * SSM_HEAD_DIM
    conv_dim = conv_w.shape[2]
    sgu_width = sgu_ln_g.shape[1]
    ngrp = sgu_w.shape[1]
    seg_widths = (2 * sgu_width, d_inner, conv_dim, 2 * d)
    dt_col = 2 * sgu_width + d_inner + conv_dim

    xp = x_prompt.reshape(batch * seq, d)
    xs = x_sample.reshape(nseq * steps, d)
    state_ssm_flat = state_ssm.reshape(depth, nseq, d_inner, SSM_STATE)

    def rows3(p):
        return p.reshape(depth, 1, p.shape[-1])

    w_in_t = jnp.swapaxes(w_in, 1, 2)
    dt_b = rows3(jnp.pad(dt_bias, ((0, 0), (0, LANES - heads))))
    w_dt_exp_t = jnp.repeat(w_in_t[:, dt_col:dt_col + heads, :], SSM_HEAD_DIM, axis=1).astype(BF16)
    dt_b_exp = rows3(jnp.repeat(dt_bias, SSM_HEAD_DIM, axis=1))
    a_log_pad = rows3(jnp.pad(a_log, ((0, 0), (0, LANES - heads))))
    a_exp = rows3(jnp.repeat(-jnp.exp(a_log), SSM_HEAD_DIM, axis=1))
    d_exp = rows3(jnp.repeat(d_skip, SSM_HEAD_DIM, axis=1))
    n1, n2, ng = rows3(norm1_g), rows3(norm2_g), rows3(ssm_norm_g)
    cb, lng, lnb = rows3(conv_b), rows3(sgu_ln_g), rows3(sgu_ln_b)
    tril = jnp.tril(jnp.ones((SGU_CHUNK, SGU_CHUNK), bool))
    mix_p = jnp.where(tril, sgu_w, 0).astype(BF16)
    bias_p = jnp.repeat(jnp.swapaxes(sgu_b, 1, 2), SGU_GROUP_DIM, axis=2)
    seqs_per_chunk = SGU_CHUNK // steps
    eye = jnp.eye(seqs_per_chunk, dtype=F32)
    w_small = jnp.where(jnp.tril(jnp.ones((steps, steps), bool)), sgu_w[:, :, :steps, :steps], 0)
    mix_s = jnp.einsum("ab,lgts->lgatbs", eye, w_small).reshape(depth, ngrp, SGU_CHUNK, SGU_CHUNK).astype(BF16)
    bias_s = jnp.tile(bias_p[:, :steps], (1, seqs_per_chunk, 1))
    wa, wb, wo = w_out_a.astype(BF16), w_out_b.astype(BF16), w_o.astype(BF16)
    wg, wu, wd = w_ffn_gate.astype(BF16), w_ffn_up.astype(BF16), w_ffn_down.astype(BF16)
    expand = (jnp.arange(d_inner)[None, :] // SSM_HEAD_DIM == jnp.arange(LANES)[:, None]).astype(BF16)
    e2 = jnp.concatenate([expand, expand], axis=0)
    rq = jnp.arange((CONV_WIDTH - 1) * SSD_CHUNK)
    shift = (jnp.arange(2 * SSD_CHUNK)[None, :]
             == (CONV_HALO - (CONV_WIDTH - 1) + rq % SSD_CHUNK + rq // SSD_CHUNK)[:, None]).astype(BF16)

    fin_g3 = final_norm_g.reshape(1, 1, d)
    w_su, w_z, w_xbc, w_gt = seg_widths

    def project(h, h_sample, l, tm):
        segs = ((0, w_su, jax.nn.gelu), (w_su, w_z, _identity), (w_su + w_z, w_xbc, _identity),
                (dt_col + heads, w_gt, jax.nn.sigmoid))
        outs = [_proj(h, w_in_t, l, c0, wdt, act, BF16, tm, 1024, extra=(h_sample, F32)) for c0, wdt, act in segs]
        return [o[0] for o in outs], [o[1] for o in outs]

    def finish(x, a, yb, gt, l, tm):
        x = _merge(a, yb, gt, x, wb, wo, l, _row_tile(x.shape[0], 256))
        if l == depth - 1:
            (y,) = _ffn(x, n2, wg, wu, wd, fin_g3, l, 0, tm, 512, True)
            return y, None
        return _ffn(x, n2, wg, wu, wd, n1, l, l + 1, tm, 512, False)

    tm_p = _row_tile(xp.shape[0], 1024)
    tm_s = _row_tile(xs.shape[0], 512)
    hp = _norm(xp, n1, 0, tm_p)
    hs = _norm(xs, n1, 0, tm_s)
    prev_p = prev_s = prev_v = None
    for l in range(depth):
        (su, z, xbc, gt), sample_proj = project(hp, hs, l, tm_p)

        dt = _proj(hp, w_in_t, l, dt_col, LANES, _softplus, F32, tm_p, LANES, bias=dt_b)
        (a,) = _sgu(su, lng, lnb, mix_p, bias_p, wa, l, _row_tile(xp.shape[0], 512), False)
        yb, *prev_p = _ssd_prompt(xbc, z, dt, conv_w, cb, a_log_pad, d_exp, ng, e2, shift, batch, seq, depth, l,
                                  prev_p)
        xp, hp = finish(xp, a, yb, gt, l, _row_tile(xp.shape[0], 512))

        su, z, xbc, gt = sample_proj
        dte = _proj(hs, w_dt_exp_t, l, 0, d_inner, _softplus, F32, tm_s, 1024, bias=dt_b_exp)
        a, *prev_v = _sgu(su, lng, lnb, mix_s, bias_s, wa, l, tm_s, True, prev_v)
        yb, *prev_s = _ssd_sample(xbc, z, dte, state_conv, state_ssm_flat, l, conv_w, cb, a_exp, d_exp, ng,
                                  steps, prev_s)
        xs, hs = finish(xs, a, yb, gt, l, tm_s)

    ssm_p, conv_p = prev_p
    ssm_s, conv_s = prev_s
    (v_s,) = prev_v
    return (xp.reshape(batch, seq, d), xs.reshape(nseq, steps, d),
            ssm_p.reshape(depth, batch, heads, SSM_HEAD_DIM, SSM_STATE), conv_p,
            ssm_s.reshape(depth, nseq, heads, SSM_HEAD_DIM, SSM_STATE), conv_s,
            v_s.reshape(depth, nseq, steps, sgu_width))
```
